```python
import math
import jax, jax.numpy as jnp
from jax import lax
import numpy as np

D_MODEL = 2048
BATCH = 4
SEQ = 8192
DEPTH = 1

MIX_WIDTH = D_MODEL
CONV_WIDTH = MIX_WIDTH // 2
CONV_GROUPS = 8
CONV_K = 3
ATTN_WIDTH = MIX_WIDTH - CONV_WIDTH
N_DIFF_HEADS = 8
DIFF_DK = ATTN_WIDTH // N_DIFF_HEADS // 2
DIFF_DV = 2 * DIFF_DK
FFN_DIM = 5632
Q_BLOCK = 128
EPS = 1e-6
IN_COLS = 3 * CONV_WIDTH + 3 * ATTN_WIDTH

kernel_name = "hybrid_conv_diffattn_macaron_block"


def rms_norm(x, g):
    xf = x.astype(jnp.float32)
    y = xf * lax.rsqrt(jnp.mean(xf * xf, axis=-1, keepdims=True) + EPS)
    return (y * g.astype(jnp.float32)).astype(x.dtype)


def swiglu(x, w_gate, w_up, w_down):
    return (jax.nn.silu(x @ w_gate) * (x @ w_up)) @ w_down


def causal_depthwise_conv(u, w):
    s = u.shape[1]
    u_pad = jnp.pad(u, ((0, 0), (CONV_K - 1, 0), (0, 0)))
    return sum(w[k] * u_pad[:, k:k + s] for k in range(CONV_K))


def short_conv_mixer(b_gate, c_gate, xv, conv_w):
    return b_gate * causal_depthwise_conv(c_gate * xv, conv_w)


def diff_attention(q, k, v, lam):
    bsz, s = q.shape[0], q.shape[1]
    nb = s // Q_BLOCK
    scale = 1.0 / math.sqrt(DIFF_DK)
    qb = q.reshape(bsz, nb, Q_BLOCK, N_DIFF_HEADS, 2, DIFF_DK).transpose(1, 0, 3, 4, 2, 5)
    kt = k.transpose(0, 2, 3, 1, 4)
    vt = v.transpose(0, 2, 1, 3)
    key_pos = jnp.arange(s)

    def block(args):
        q_blk, i = args
        scores = jnp.einsum('bhcqd,bhckd->bhcqk', q_blk, kt).astype(jnp.float32) * scale
        q_pos = i * Q_BLOCK + jnp.arange(Q_BLOCK)
        mask = key_pos[None, :] <= q_pos[:, None]
        scores = jnp.where(mask, scores, -jnp.inf)
        p = jax.nn.softmax(scores, axis=-1)
        a = p[:, :, 0] - lam * p[:, :, 1]
        return jnp.einsum('bhqk,bhkd->bhqd', a.astype(vt.dtype), vt)

    o = lax.map(block, (qb, jnp.arange(nb)))
    return o.transpose(1, 0, 3, 2, 4).reshape(bsz, s, N_DIFF_HEADS, DIFF_DV)


def setup_inputs(seed: int = 0) -> dict:
    key = jax.random.key(seed)
    ks = jax.random.split(key, 20)
    f32 = jnp.float32

    def w(k, shape, fan_in):
        return jax.random.normal(k, shape, f32) * fan_in ** -0.5

    def gain(k, shape):
        return 1.0 + 0.05 * jax.random.normal(k, shape, f32)

    return {
        "x": jax.random.normal(ks[0], (BATCH, SEQ, D_MODEL), f32),
        "ffn1_norm": gain(ks[1], (DEPTH, D_MODEL)),
        "ffn1_w_gate": w(ks[2], (DEPTH, D_MODEL, FFN_DIM), D_MODEL),
        "ffn1_w_up": w(ks[3], (DEPTH, D_MODEL, FFN_DIM), D_MODEL),
        "ffn1_w_down": w(ks[4], (DEPTH, FFN_DIM, D_MODEL), FFN_DIM),
        "mix_norm": gain(ks[5], (DEPTH, D_MODEL)),
        "w_in": w(ks[6], (DEPTH, D_MODEL, IN_COLS), D_MODEL),
        "conv_w": w(ks[7], (DEPTH, CONV_K, CONV_WIDTH), CONV_K),
        "lambda_q1": 0.1 * jax.random.normal(ks[8], (DEPTH, DIFF_DK), f32),
        "lambda_k1": 0.1 * jax.random.normal(ks[9], (DEPTH, DIFF_DK), f32),
        "lambda_q2": 0.1 * jax.random.normal(ks[10], (DEPTH, DIFF_DK), f32),
        "lambda_k2": 0.1 * jax.random.normal(ks[11], (DEPTH, DIFF_DK), f32),
        "subln_w": gain(ks[12], (DEPTH, DIFF_DV)),
        "w_out": w(ks[13], (DEPTH, MIX_WIDTH, D_MODEL), MIX_WIDTH),
        "ffn2_norm": gain(ks[14], (DEPTH, D_MODEL)),
        "ffn2_w_gate": w(ks[15], (DEPTH, D_MODEL, FFN_DIM), D_MODEL),
        "ffn2_w_up": w(ks[16], (DEPTH, D_MODEL, FFN_DIM), D_MODEL),
        "ffn2_w_down": w(ks[17], (DEPTH, FFN_DIM, D_MODEL), FFN_DIM),
        "final_norm": gain(ks[18], (D_MODEL,)),
    }


def reference(x, ffn1_norm, ffn1_w_gate, ffn1_w_up, ffn1_w_down, mix_norm, w_in,
              conv_w, lambda_q1, lambda_k1, lambda_q2, lambda_k2, subln_w, w_out,
              ffn2_norm, ffn2_w_gate, ffn2_w_up, ffn2_w_down, final_norm):
    bsz, s, _ = x.shape
    for l in range(DEPTH):
        x = x + 0.5 * swiglu(rms_norm(x, ffn1_norm[l]), ffn1_w_gate[l], ffn1_w_up[l], ffn1_w_down[l])

        h = rms_norm(x, mix_norm[l])
        proj = h @ w_in[l]
        b_gate, c_gate, xv, q, k, v = jnp.split(
            proj, np.cumsum([CONV_WIDTH] * 3 + [ATTN_WIDTH] * 2).tolist(), axis=-1)

        y_conv = short_conv_mixer(b_gate, c_gate, xv, conv_w[l])

        lambda_init = 0.8 - 0.6 * math.exp(-0.3 * l)
        lam = (jnp.exp(jnp.sum(lambda_q1[l].astype(jnp.float32) * lambda_k1[l].astype(jnp.float32)))
               - jnp.exp(jnp.sum(lambda_q2[l].astype(jnp.float32) * lambda_k2[l].astype(jnp.float32)))
               + lambda_init)
        q = q.reshape(bsz, s, N_DIFF_HEADS, 2, DIFF_DK)
        k = k.reshape(bsz, s, N_DIFF_HEADS, 2, DIFF_DK)
        v = v.reshape(bsz, s, N_DIFF_HEADS, DIFF_DV)
        o = diff_attention(q, k, v, lam)
        o = rms_norm(o, subln_w[l]) * (1.0 - lambda_init)
        y_attn = o.reshape(bsz, s, ATTN_WIDTH)

        x = x + jnp.concatenate([y_conv, y_attn], axis=-1) @ w_out[l]

        x = x + 0.5 * swiglu(rms_norm(x, ffn2_norm[l]), ffn2_w_gate[l], ffn2_w_up[l], ffn2_w_down[l])

    return rms_norm(x, final_norm)
```

```python
import functools
import math

import jax
import jax.numpy as jnp
from jax import lax
from jax.experimental import pallas as pl
from jax.experimental.pallas import tpu as pltpu

EPS = 1e-6
CONV_K = 3
N_HEADS = 8
DK = 64
DV = 128
LANES = 128
BF16_ROWS = 16
MIB = 1024 * 1024

F32 = jnp.float32
BF16 = jnp.bfloat16


def _rms(x, g):
    return x * lax.rsqrt(jnp.mean(x * x, axis=-1, keepdims=True) + EPS) * g


def _ffn_kernel(x_ref, g_ref, wg_ref, wu_ref, wd_ref, fg_ref, o_ref, n_ref, *, final_norm):
    f = pl.program_id(1)

    @pl.when(f == 0)
    def _():
        n_ref[...] = _rms(x_ref[...], g_ref[...]).astype(BF16)
        o_ref[...] = jnp.zeros_like(o_ref)

    n = n_ref[...]
    gate = jnp.dot(n, wg_ref[...], preferred_element_type=F32)
    up = jnp.dot(n, wu_ref[...], preferred_element_type=F32)
    h = (gate * jax.nn.sigmoid(gate) * up).astype(BF16)
    o_ref[...] += jnp.dot(h, wd_ref[...], preferred_element_type=F32)

    @pl.when(f == pl.num_programs(1) - 1)
    def _():
        y = x_ref[...] + 0.5 * o_ref[...]
        if final_norm:
            y = _rms(y, fg_ref[...])
        o_ref[...] = y


def _ffn(x, gain, wg, wu, wd, final_gain, *, final_norm, tm=512, tf=512):
    n_tok, d = x.shape
    ffn = wg.shape[1]
    return pl.pallas_call(
        functools.partial(_ffn_kernel, final_norm=final_norm),
        grid=(n_tok // tm, ffn // tf),
        in_specs=[
            pl.BlockSpec((tm, d), lambda i, f: (i, 0)),
            pl.BlockSpec((1, d), lambda i, f: (0, 0)),
            pl.BlockSpec((d, tf), lambda i, f: (0, f)),
            pl.BlockSpec((d, tf), lambda i, f: (0, f)),
            pl.BlockSpec((tf, d), lambda i, f: (f, 0)),
            pl.BlockSpec((1, d), lambda i, f: (0, 0)),
        ],
        out_specs=pl.BlockSpec((tm, d), lambda i, f: (i, 0)),
        out_shape=jax.ShapeDtypeStruct((n_tok, d), F32),
        scratch_shapes=[pltpu.VMEM((tm, d), BF16)],
        compiler_params=pltpu.CompilerParams(
            dimension_semantics=("arbitrary", "arbitrary"),
            vmem_limit_bytes=48 * MIB),
        name="ffn",
    )(x, gain, wg, wu, wd, final_gain)


def _in_proj_kernel(x_ref, g_ref, w_ref, o_ref, n_ref):
    @pl.when(pl.program_id(1) == 0)
    def _():
        n_ref[...] = _rms(x_ref[...], g_ref[...]).astype(BF16)

    o_ref[...] = jnp.dot(n_ref[...], w_ref[...], preferred_element_type=F32).astype(o_ref.dtype)


def _in_proj(x, gain, w, *, tm=1024, tn=512):
    n_tok, d = x.shape
    cols = w.shape[1]
    return pl.pallas_call(
        _in_proj_kernel,
        grid=(n_tok // tm, cols // tn),
        in_specs=[
            pl.BlockSpec((tm, d), lambda i, j: (i, 0)),
            pl.BlockSpec((1, d), lambda i, j: (0, 0)),
            pl.BlockSpec((d, tn), lambda i, j: (0, j)),
        ],
        out_specs=pl.BlockSpec((tm, tn), lambda i, j: (i, j)),
        out_shape=jax.ShapeDtypeStruct((n_tok, cols), BF16),
        scratch_shapes=[pltpu.VMEM((tm, d), BF16)],
        compiler_params=pltpu.CompilerParams(
            dimension_semantics=("arbitrary", "arbitrary"),
            vmem_limit_bytes=48 * MIB),
        name="in_proj",
    )(x, gain, w)


def _conv_taps(cx_m2, cx_m1, cx, w_ref):
    return w_ref[0:1, :] * cx_m2 + w_ref[1:2, :] * cx_m1 + w_ref[2:3, :] * cx


def _conv_kernel(b_ref, c_ref, x_ref, ch_ref, xh_ref, w_ref, o_ref, *, tiles_per_seq):
    cx = c_ref[...].astype(F32) * x_ref[...].astype(F32)
    y = _conv_taps(pltpu.roll(cx, 2, axis=0), pltpu.roll(cx, 1, axis=0), cx, w_ref)
    o_ref[...] = (b_ref[...].astype(F32) * y).astype(o_ref.dtype)

    h = BF16_ROWS
    at_seq_start = pl.program_id(0) % tiles_per_seq == 0
    halo = ch_ref[...].astype(F32) * xh_ref[...].astype(F32)
    halo = jnp.where(at_seq_start, 0.0, halo)
    row = lax.broadcasted_iota(jnp.int32, (h, 1), 0)
    top = cx[:h]
    m1 = jnp.where(row < 1, pltpu.roll(halo, 1, axis=0), pltpu.roll(top, 1, axis=0))
    m2 = jnp.where(row < 2, pltpu.roll(halo, 2, axis=0), pltpu.roll(top, 2, axis=0))
    y_top = _conv_taps(m2, m1, top, w_ref)
    o_ref[0:h, :] = (b_ref[0:h, :].astype(F32) * y_top).astype(o_ref.dtype)


def _conv(proj, conv_w, seq, *, tm=512):
    n_tok = proj.shape[0]
    cw = conv_w.shape[1]
    halo_blocks = tm // BF16_ROWS

    def halo_map(sec):
        return lambda i: (jnp.maximum(i * halo_blocks - 1, 0), sec)

    return pl.pallas_call(
        functools.partial(_conv_kernel, tiles_per_seq=seq // tm),
        grid=(n_tok // tm,),
        in_specs=[
            pl.BlockSpec((tm, cw), lambda i: (i, 0)),
            pl.BlockSpec((tm, cw), lambda i: (i, 1)),
            pl.BlockSpec((tm, cw), lambda i: (i, 2)),
            pl.BlockSpec((BF16_ROWS, cw), halo_map(1)),
            pl.BlockSpec((BF16_ROWS, cw), halo_map(2)),
            pl.BlockSpec((CONV_K, cw), lambda i: (0, 0)),
        ],
        out_specs=pl.BlockSpec((tm, cw), lambda i: (i, 0)),
        out_shape=jax.ShapeDtypeStruct((n_tok, cw), BF16),
        compiler_params=pltpu.CompilerParams(dimension_semantics=("arbitrary",)),
        name="conv",
    )(proj, proj, proj, proj, proj, conv_w)


def _attn_kernel(lq1_ref, lk1_ref, lq2_ref, lk2_ref, sw_ref, q_ref, k_ref, v_ref, o_ref,
                 q2_ref, m_ref, l_ref, acc_ref, *, tq, lambda_init):
    i = pl.program_id(2)
    scale = 1.0 / math.sqrt(DK)

    q = q_ref[...]
    lane = lax.broadcasted_iota(jnp.int32, q.shape, 1)
    qs = (q.astype(F32) * scale).astype(BF16)
    zero = jnp.zeros_like(qs)
    q2_ref[0:tq, :] = jnp.where(lane < DK, qs, zero)
    q2_ref[tq:2 * tq, :] = jnp.where(lane >= DK, qs, zero)
    m_ref[...] = jnp.full_like(m_ref, -jnp.inf)
    l_ref[...] = jnp.zeros_like(l_ref)
    acc_ref[...] = jnp.zeros_like(acc_ref)

    def step(j, masked):
        k = k_ref[pl.ds(pl.multiple_of(j * tq, tq), tq), :]
        v = v_ref[pl.ds(pl.multiple_of(j * tq, tq), tq), :]
        s = lax.dot_general(q2_ref[...], k, (((1,), (1,)), ((), ())), preferred_element_type=F32)
        if masked:
            r = lax.broadcasted_iota(jnp.int32, (2 * tq, tq), 0)
            r = jnp.where(r >= tq, r - tq, r)
            c = lax.broadcasted_iota(jnp.int32, (2 * tq, tq), 1)
            s = jnp.where(c <= r, s, -jnp.inf)
        m_prev = m_ref[...]
        m_new = jnp.maximum(m_prev, jnp.max(s, axis=-1, keepdims=True))
        alpha = jnp.exp(m_prev - m_new)
        p = jnp.exp(s - m_new)
        l_ref[...] = alpha * l_ref[...] + jnp.sum(p, axis=-1, keepdims=True)
        acc_ref[...] = alpha * acc_ref[...] + jnp.dot(p.astype(BF16), v, preferred_element_type=F32)
        m_ref[...] = m_new

    def body(j, carry):
        step(j, masked=False)
        return carry

    lax.fori_loop(0, i, body, 0)
    step(i, masked=True)

    lam = (jnp.exp(jnp.sum(lq1_ref[...] * lk1_ref[...], axis=-1, keepdims=True))
           - jnp.exp(jnp.sum(lq2_ref[...] * lk2_ref[...], axis=-1, keepdims=True))
           + lambda_init)
    o_all = acc_ref[...] / l_ref[...]
    o = o_all[0:tq] - lam * o_all[tq:2 * tq]
    o_ref[...] = (_rms(o, sw_ref[...]) * (1.0 - lambda_init)).astype(o_ref.dtype)


def _attn(proj, lq1, lk1, lq2, lk2, subln, batch, seq, lambda_init, *, tq=512):
    n_tok = proj.shape[0]
    nq = seq // tq
    q_blk = 3 * (N_HEADS * DV) // LANES
    k_blk = q_blk + N_HEADS
    v_blk = k_blk + N_HEADS
    vec = lambda width: pl.BlockSpec((1, width), lambda b, h, i: (0, 0))
    return pl.pallas_call(
        functools.partial(_attn_kernel, tq=tq, lambda_init=lambda_init),
        grid=(batch, N_HEADS, nq),
        in_specs=[
            vec(DK), vec(DK), vec(DK), vec(DK), vec(DV),
            pl.BlockSpec((tq, LANES), lambda b, h, i: (b * nq + i, q_blk + h)),
            pl.BlockSpec((seq, LANES), lambda b, h, i: (b, k_blk + h)),
            pl.BlockSpec((seq, LANES), lambda b, h, i: (b, v_blk + h)),
        ],
        out_specs=pl.BlockSpec((tq, DV), lambda b, h, i: (b * nq + i, h)),
        out_shape=jax.ShapeDtypeStruct((n_tok, N_HEADS * DV), BF16),
        scratch_shapes=[
            pltpu.VMEM((2 * tq, LANES), BF16),
            pltpu.VMEM((2 * tq, 1), F32),
            pltpu.VMEM((2 * tq, 1), F32),
            pltpu.VMEM((2 * tq, DV), F32),
        ],
        compiler_params=pltpu.CompilerParams(
            dimension_semantics=("arbitrary", "arbitrary", "arbitrary"),
            vmem_limit_bytes=48 * MIB),
        name="diff_attn",
    )(lq1, lk1, lq2, lk2, subln, proj, proj, proj)


def _out_proj_kernel(x_ref, yc_ref, ya_ref, wc_ref, wa_ref, o_ref):
    y = jnp.dot(yc_ref[...], wc_ref[...], preferred_element_type=F32)
    y += jnp.dot(ya_ref[...], wa_ref[...], preferred_element_type=F32)
    o_ref[...] = x_ref[...] + y


def _out_proj(x, y_conv, y_attn, w_out, *, tm=512):
    n_tok, d = x.shape
    cw = y_conv.shape[1]
    aw = y_attn.shape[1]
    return pl.pallas_call(
        _out_proj_kernel,
        grid=(n_tok // tm,),
        in_specs=[
            pl.BlockSpec((tm, d), lambda i: (i, 0)),
            pl.BlockSpec((tm, cw), lambda i: (i, 0)),
            pl.BlockSpec((tm, aw), lambda i: (i, 0)),
            pl.BlockSpec((cw, d), lambda i: (0, 0)),
            pl.BlockSpec((aw, d), lambda i: (1, 0)),
        ],
        out_specs=pl.BlockSpec((tm, d), lambda i: (i, 0)),
        out_shape=jax.ShapeDtypeStruct((n_tok, d), F32),
        compiler_params=pltpu.CompilerParams(
            dimension_semantics=("arbitrary",),
            vmem_limit_bytes=48 * MIB),
        name="out_proj",
    )(x, y_conv, y_attn, w_out, w_out)


def kernel(x, ffn1_norm, ffn1_w_gate, ffn1_w_up, ffn1_w_down, mix_norm, w_in, conv_w, lambda_q1, lambda_k1, lambda_q2, lambda_k2, subln_w, w_out, ffn2_norm, ffn2_w_gate, ffn2_w_up, ffn2_w_down, final_norm):
    batch, seq, d = x.shape
    depth = ffn1_norm.shape[0]
    assert w_in.shape[2] == 3 * conv_w.shape[2] + 3 * N_HEADS * DV
    assert conv_w.shape[2] == N_HEADS * DV
    h = x.reshape(batch * seq, d)
    row = lambda a: a.reshape(1, -1)
    for l in range(depth):
        last = l == depth - 1
        lambda_init = 0.8 - 0.6 * math.exp(-0.3 * l)
        h = _ffn(h, row(ffn1_norm[l]), ffn1_w_gate[l].astype(BF16), ffn1_w_up[l].astype(BF16),
                 ffn1_w_down[l].astype(BF16), row(final_norm), final_norm=False)
        proj = _in_proj(h, row(mix_norm[l]), w_in[l].astype(BF16))
        y_conv = _conv(proj, conv_w[l], seq)
        y_attn = _attn(proj, row(lambda_q1[l]), row(lambda_k1[l]), row(lambda_q2[l]), row(lambda_k2[l]),
                       row(subln_w[l]), batch, seq, lambda_init)
        h = _out_proj(h, y_conv, y_attn, w_out[l].astype(BF16))
        h = _ffn(h, row(ffn2_norm[l]), ffn2_w_gate[l].astype(BF16), ffn2_w_up[l].astype(BF16),
                 ffn2_w_down[l].astype(BF16), row(final_norm), final_norm=last)
    if depth == 0:
        raise ValueError("depth must be at least 1")
    return h.reshape(batch, seq, d)
```

```python
import functools
import math

import jax
import jax.numpy as jnp
from jax import lax
from jax.experimental import pallas as pl
from jax.experimental.pallas import tpu as pltpu

EPS = 1e-6
CONV_K = 3
N_HEADS = 8
DK = 64
DV = 128
LANES = 128
BF16_ROWS = 16
MIB = 1024 * 1024

F32 = jnp.float32
BF16 = jnp.bfloat16


def _rms(x, g):
    return x * lax.rsqrt(jnp.mean(x * x, axis=-1, keepdims=True) + EPS) * g


def _ffn_kernel(x_ref, g_ref, wg_ref, wu_ref, wd_ref, fg_ref, o_ref, n_ref, *, final_norm):
    f = pl.program_id(1)

    @pl.when(f == 0)
    def _():
        n_ref[...] = _rms(x_ref[...], g_ref[...]).astype(BF16)
        o_ref[...] = jnp.zeros_like(o_ref)

    n = n_ref[...]
    gate = jnp.dot(n, wg_ref[...], preferred_element_type=F32)
    up = jnp.dot(n, wu_ref[...], preferred_element_type=F32)
    h = (gate * jax.nn.sigmoid(gate) * up).astype(BF16)
    o_ref[...] += jnp.dot(h, wd_ref[...], preferred_element_type=F32)

    @pl.when(f == pl.num_programs(1) - 1)
    def _():
        y = x_ref[...] + 0.5 * o_ref[...]
        if final_norm:
            y = _rms(y, fg_ref[...])
        o_ref[...] = y


def _ffn(x, gain, wg, wu, wd, final_gain, *, final_norm, tm=512, tf=512):
    n_tok, d = x.shape
    ffn = wg.shape[1]
    return pl.pallas_call(
        functools.partial(_ffn_kernel, final_norm=final_norm),
        grid=(n_tok // tm, ffn // tf),
        in_specs=[
            pl.BlockSpec((tm, d), lambda i, f: (i, 0)),
            pl.BlockSpec((1, d), lambda i, f: (0, 0)),
            pl.BlockSpec((d, tf), lambda i, f: (0, f)),
            pl.BlockSpec((d, tf), lambda i, f: (0, f)),
            pl.BlockSpec((tf, d), lambda i, f: (f, 0)),
            pl.BlockSpec((1, d), lambda i, f: (0, 0)),
        ],
        out_specs=pl.BlockSpec((tm, d), lambda i, f: (i, 0)),
        out_shape=jax.ShapeDtypeStruct((n_tok, d), F32),
        scratch_shapes=[pltpu.VMEM((tm, d), BF16)],
        compiler_params=pltpu.CompilerParams(
            dimension_semantics=("arbitrary", "arbitrary"),
            vmem_limit_bytes=48 * MIB),
        name="ffn",
    )(x, gain, wg, wu, wd, final_gain)


def _in_proj_kernel(x_ref, g_ref, w_ref, o_ref, n_ref):
    @pl.when(pl.program_id(1) == 0)
    def _():
        n_ref[...] = _rms(x_ref[...], g_ref[...]).astype(BF16)

    o_ref[...] = jnp.dot(n_ref[...], w_ref[...], preferred_element_type=F32).astype(o_ref.dtype)


def _in_proj(x, gain, w, *, tm=1024, tn=512):
    n_tok, d = x.shape
    cols = w.shape[1]
    return pl.pallas_call(
        _in_proj_kernel,
        grid=(n_tok // tm, cols // tn),
        in_specs=[
            pl.BlockSpec((tm, d), lambda i, j: (i, 0)),
            pl.BlockSpec((1, d), lambda i, j: (0, 0)),
            pl.BlockSpec((d, tn), lambda i, j: (0, j)),
        ],
        out_specs=pl.BlockSpec((tm, tn), lambda i, j: (i, j)),
        out_shape=jax.ShapeDtypeStruct((n_tok, cols), BF16),
        scratch_shapes=[pltpu.VMEM((tm, d), BF16)],
        compiler_params=pltpu.CompilerParams(
            dimension_semantics=("arbitrary", "arbitrary"),
            vmem_limit_bytes=48 * MIB),
        name="in_proj",
    )(x, gain, w)


def _conv_taps(cx_m2, cx_m1, cx, w_ref):
    return w_ref[0:1, :] * cx_m2 + w_ref[1:2, :] * cx_m1 + w_ref[2:3, :] * cx


def _conv_kernel(b_ref, c_ref, x_ref, ch_ref, xh_ref, w_ref, o_ref, *, tiles_per_seq):
    cx = c_ref[...].astype(F32) * x_ref[...].astype(F32)
    y = _conv_taps(pltpu.roll(cx, 2, axis=0), pltpu.roll(cx, 1, axis=0), cx, w_ref)
    o_ref[...] = (b_ref[...].astype(F32) * y).astype(o_ref.dtype)

    h = BF16_ROWS
    at_seq_start = pl.program_id(0) % tiles_per_seq == 0
    halo = ch_ref[...].astype(F32) * xh_ref[...].astype(F32)
    halo = jnp.where(at_seq_start, 0.0, halo)
    row = lax.broadcasted_iota(jnp.int32, (h, 1), 0)
    top = cx[:h]
    m1 = jnp.where(row < 1, pltpu.roll(halo, 1, axis=0), pltpu.roll(top, 1, axis=0))
    m2 = jnp.where(row < 2, pltpu.roll(halo, 2, axis=0), pltpu.roll(top, 2, axis=0))
    y_top = _conv_taps(m2, m1, top, w_ref)
    o_ref[0:h, :] = (b_ref[0:h, :].astype(F32) * y_top).astype(o_ref.dtype)


def _conv(proj, conv_w, seq, *, tm=512):
    n_tok = proj.shape[0]
    cw = conv_w.shape[1]
    halo_blocks = tm // BF16_ROWS

    def halo_map(sec):
        return lambda i: (jnp.maximum(i * halo_blocks - 1, 0), sec)

    return pl.pallas_call(
        functools.partial(_conv_kernel, tiles_per_seq=seq // tm),
        grid=(n_tok // tm,),
        in_specs=[
            pl.BlockSpec((tm, cw), lambda i: (i, 0)),
            pl.BlockSpec((tm, cw), lambda i: (i, 1)),
            pl.BlockSpec((tm, cw), lambda i: (i, 2)),
            pl.BlockSpec((BF16_ROWS, cw), halo_map(1)),
            pl.BlockSpec((BF16_ROWS, cw), halo_map(2)),
            pl.BlockSpec((CONV_K, cw), lambda i: (0, 0)),
        ],
        out_specs=pl.BlockSpec((tm, cw), lambda i: (i, 0)),
        out_shape=jax.ShapeDtypeStruct((n_tok, cw), BF16),
        compiler_params=pltpu.CompilerParams(dimension_semantics=("arbitrary",)),
        name="conv",
    )(proj, proj, proj, proj, proj, conv_w)


def _attn_kernel(lq1_ref, lk1_ref, lq2_ref, lk2_ref, sw_ref, q_ref, k_ref, v_ref, o_ref,
                 q2_ref, m_ref, l_ref, acc_ref, *, tq, rc, kv_unroll, lambda_init):
    i = pl.program_id(2)
    scale = 1.0 / math.sqrt(DK)

    q = q_ref[...]
    lane = lax.broadcasted_iota(jnp.int32, q.shape, 1)
    qs = (q.astype(F32) * scale).astype(BF16)
    zero = jnp.zeros_like(qs)
    q2_ref[0:tq, :] = jnp.where(lane < DK, qs, zero)
    q2_ref[tq:2 * tq, :] = jnp.where(lane >= DK, qs, zero)
    m_ref[...] = jnp.full_like(m_ref, -jnp.inf)
    l_ref[...] = jnp.zeros_like(l_ref)
    acc_ref[...] = jnp.zeros_like(acc_ref)

    def step(col0, tk, masked):
        k = k_ref[pl.ds(col0, tk), :]
        v = v_ref[pl.ds(col0, tk), :]
        v_ones = jnp.concatenate([v, jnp.ones_like(v)], axis=1)
        for r0 in range(0, 2 * tq, rc):
            rows = pl.ds(r0, rc)
            s = lax.dot_general(q2_ref[rows, :], k, (((1,), (1,)), ((), ())), preferred_element_type=F32)
            if masked:
                q_pos = lax.broadcasted_iota(jnp.int32, (rc, tk), 0) + (r0 % tq)
                k_pos = lax.broadcasted_iota(jnp.int32, (rc, tk), 1)
                s = jnp.where(k_pos <= q_pos, s, -jnp.inf)
            cols = [s[:, c0:c0 + LANES] for c0 in range(0, tk, LANES)]
            lane_max = functools.reduce(jnp.maximum, cols)
            row_max = jnp.broadcast_to(jnp.max(lane_max, axis=-1, keepdims=True), (rc, LANES))
            m_prev = m_ref[rows, :]
            m_new = jnp.maximum(m_prev, row_max)
            alpha = jnp.exp(m_prev - m_new)
            p = jnp.concatenate([jnp.exp(col - m_new) for col in cols], axis=1).astype(BF16)
            pv = jnp.dot(p, v_ones, preferred_element_type=F32)
            acc_ref[rows, :] = alpha * acc_ref[rows, :] + pv[:, :DV]
            l_ref[rows, :] = alpha * l_ref[rows, :] + pv[:, DV:]
            m_ref[rows, :] = m_new

    wide = kv_unroll * tq

    def body(j, carry):
        step(pl.multiple_of(j * wide, wide), wide, masked=False)
        return carry

    n_wide = i // kv_unroll
    lax.fori_loop(0, n_wide, body, 0)
    for t in range(kv_unroll - 1):
        @pl.when(i - n_wide * kv_unroll > t)
        def _():
            step(pl.multiple_of((n_wide * kv_unroll + t) * tq, tq), tq, masked=False)
    step(pl.multiple_of(i * tq, tq), tq, masked=True)

    lam = (jnp.exp(jnp.sum(lq1_ref[...] * lk1_ref[...], axis=-1, keepdims=True))
           - jnp.exp(jnp.sum(lq2_ref[...] * lk2_ref[...], axis=-1, keepdims=True))
           + lambda_init)
    o_all = acc_ref[...] / l_ref[...]
    o = o_all[0:tq] - lam * o_all[tq:2 * tq]
    o_ref[...] = (_rms(o, sw_ref[...]) * (1.0 - lambda_init)).astype(o_ref.dtype)


def _attn(proj, lq1, lk1, lq2, lk2, subln, batch, seq, lambda_init, *, tq=512, rc=128, kv_unroll=2):
    n_tok = proj.shape[0]
    nq = seq // tq
    q_blk = 3 * (N_HEADS * DV) // LANES
    k_blk = q_blk + N_HEADS
    v_blk = k_blk + N_HEADS
    vec = lambda width: pl.BlockSpec((1, width), lambda b, h, i: (0, 0))
    return pl.pallas_call(
        functools.partial(_attn_kernel, tq=tq, rc=rc, kv_unroll=kv_unroll, lambda_init=lambda_init),
        grid=(batch, N_HEADS, nq),
        in_specs=[
            vec(DK), vec(DK), vec(DK), vec(DK), vec(DV),
            pl.BlockSpec((tq, LANES), lambda b, h, i: (b * nq + i, q_blk + h)),
            pl.BlockSpec((seq, LANES), lambda b, h, i: (b, k_blk + h)),
            pl.BlockSpec((seq, LANES), lambda b, h, i: (b, v_blk + h)),
        ],
        out_specs=pl.BlockSpec((tq, DV), lambda b, h, i: (b * nq + i, h)),
        out_shape=jax.ShapeDtypeStruct((n_tok, N_HEADS * DV), BF16),
        scratch_shapes=[
            pltpu.VMEM((2 * tq, LANES), BF16),
            pltpu.VMEM((2 * tq, LANES), F32),
            pltpu.VMEM((2 * tq, LANES), F32),
            pltpu.VMEM((2 * tq, DV), F32),
        ],
        compiler_params=pltpu.CompilerParams(
            dimension_semantics=("arbitrary", "arbitrary", "arbitrary"),
            vmem_limit_bytes=48 * MIB),
        name="diff_attn",
    )(lq1, lk1, lq2, lk2, subln, proj, proj, proj)


def _out_proj_kernel(x_ref, yc_ref, ya_ref, wc_ref, wa_ref, o_ref):
    y = jnp.dot(yc_ref[...], wc_ref[...], preferred_element_type=F32)
    y += jnp.dot(ya_ref[...], wa_ref[...], preferred_element_type=F32)
    o_ref[...] = x_ref[...] + y


def _out_proj(x, y_conv, y_attn, w_out, *, tm=512):
    n_tok, d = x.shape
    cw = y_conv.shape[1]
    aw = y_attn.shape[1]
    return pl.pallas_call(
        _out_proj_kernel,
        grid=(n_tok // tm,),
        in_specs=[
            pl.BlockSpec((tm, d), lambda i: (i, 0)),
            pl.BlockSpec((tm, cw), lambda i: (i, 0)),
            pl.BlockSpec((tm, aw), lambda i: (i, 0)),
            pl.BlockSpec((cw, d), lambda i: (0, 0)),
            pl.BlockSpec((aw, d), lambda i: (1, 0)),
        ],
        out_specs=pl.BlockSpec((tm, d), lambda i: (i, 0)),
        out_shape=jax.ShapeDtypeStruct((n_tok, d), F32),
        compiler_params=pltpu.CompilerParams(
            dimension_semantics=("arbitrary",),
            vmem_limit_bytes=48 * MIB),
        name="out_proj",
    )(x, y_conv, y_attn, w_out, w_out)


def kernel(x, ffn1_norm, ffn1_w_gate, ffn1_w_up, ffn1_w_down, mix_norm, w_in, conv_w, lambda_q1, lambda_k1, lambda_q2, lambda_k2, subln_w, w_out, ffn2_norm, ffn2_w_gate, ffn2_w_up, ffn2_w_down, final_norm):
    batch, seq, d = x.shape
    depth = ffn1_norm.shape[0]
    assert w_in.shape[2] == 3 * conv_w.shape[2] + 3 * N_HEADS * DV
    assert conv_w.shape[2] == N_HEADS * DV
    h = x.reshape(batch * seq, d)
    row = lambda a: a.reshape(1, -1)
    for l in range(depth):
        last = l == depth - 1
        lambda_init = 0.8 - 0.6 * math.exp(-0.3 * l)
        h = _ffn(h, row(ffn1_norm[l]), ffn1_w_gate[l].astype(BF16), ffn1_w_up[l].astype(BF16),
                 ffn1_w_down[l].astype(BF16), row(final_norm), final_norm=False)
        proj = _in_proj(h, row(mix_norm[l]), w_in[l].astype(BF16))
        y_conv = _conv(proj, conv_w[l], seq)
        y_attn = _attn(proj, row(lambda_q1[l]), row(lambda_k1[l]), row(lambda_q2[l]), row(lambda_k2[l]),
                       row(subln_w[l]), batch, seq, lambda_init)
        h = _out_proj(h, y_conv, y_attn, w_out[l].astype(BF16))
        h = _ffn(h, row(ffn2_norm[l]), ffn2_w_gate[l].astype(BF16), ffn2_w_up[l].astype(BF16),
                 ffn2_w_down[l].astype(BF16), row(final_norm), final_norm=last)
    if depth == 0:
        raise ValueError("depth must be at least 1")
    return h.reshape(batch, seq, d)
```

```python
import functools
import math

import jax
import jax.numpy as jnp
from jax import lax
from jax.experimental import pallas as pl
from jax.experimental.pallas import tpu as pltpu

EPS = 1e-6
CONV_K = 3
N_HEADS = 8
DK = 64
DV = 128
LANES = 128
BF16_ROWS = 16
MIB = 1024 * 1024

F32 = jnp.float32
BF16 = jnp.bfloat16


def _rms(x, g):
    return x * lax.rsqrt(jnp.mean(x * x, axis=-1, keepdims=True) + EPS) * g


def _ffn_kernel(x_ref, g_ref, wg_ref, wu_ref, wd_ref, fg_ref, o_ref, n_ref, *, final_norm):
    f = pl.program_id(1)

    @pl.when(f == 0)
    def _():
        n_ref[...] = _rms(x_ref[...], g_ref[...]).astype(BF16)
        o_ref[...] = jnp.zeros_like(o_ref)

    n = n_ref[...]
    gate = jnp.dot(n, wg_ref[...], preferred_element_type=F32)
    up = jnp.dot(n, wu_ref[...], preferred_element_type=F32)
    h = (gate * jax.nn.sigmoid(gate) * up).astype(BF16)
    o_ref[...] += jnp.dot(h, wd_ref[...], preferred_element_type=F32)

    @pl.when(f == pl.num_programs(1) - 1)
    def _():
        y = x_ref[...] + 0.5 * o_ref[...]
        if final_norm:
            y = _rms(y, fg_ref[...])
        o_ref[...] = y


def _ffn(x, gain, wg, wu, wd, final_gain, *, final_norm, tm=512, tf=512):
    n_tok, d = x.shape
    ffn = wg.shape[1]
    return pl.pallas_call(
        functools.partial(_ffn_kernel, final_norm=final_norm),
        grid=(n_tok // tm, ffn // tf),
        in_specs=[
            pl.BlockSpec((tm, d), lambda i, f: (i, 0)),
            pl.BlockSpec((1, d), lambda i, f: (0, 0)),
            pl.BlockSpec((d, tf), lambda i, f: (0, f)),
            pl.BlockSpec((d, tf), lambda i, f: (0, f)),
            pl.BlockSpec((tf, d), lambda i, f: (f, 0)),
            pl.BlockSpec((1, d), lambda i, f: (0, 0)),
        ],
        out_specs=pl.BlockSpec((tm, d), lambda i, f: (i, 0)),
        out_shape=jax.ShapeDtypeStruct((n_tok, d), F32),
        scratch_shapes=[pltpu.VMEM((tm, d), BF16)],
        compiler_params=pltpu.CompilerParams(
            dimension_semantics=("arbitrary", "arbitrary"),
            vmem_limit_bytes=48 * MIB),
        name="ffn",
    )(x, gain, wg, wu, wd, final_gain)


def _in_proj_kernel(x_ref, g_ref, w_ref, o_ref, n_ref):
    @pl.when(pl.program_id(1) == 0)
    def _():
        n_ref[...] = _rms(x_ref[...], g_ref[...]).astype(BF16)

    o_ref[...] = jnp.dot(n_ref[...], w_ref[...], preferred_element_type=F32).astype(o_ref.dtype)


def _in_proj(x, gain, w, *, tm=1024, tn=512):
    n_tok, d = x.shape
    cols = w.shape[1]
    return pl.pallas_call(
        _in_proj_kernel,
        grid=(n_tok // tm, cols // tn),
        in_specs=[
            pl.BlockSpec((tm, d), lambda i, j: (i, 0)),
            pl.BlockSpec((1, d), lambda i, j: (0, 0)),
            pl.BlockSpec((d, tn), lambda i, j: (0, j)),
        ],
        out_specs=pl.BlockSpec((tm, tn), lambda i, j: (i, j)),
        out_shape=jax.ShapeDtypeStruct((n_tok, cols), BF16),
        scratch_shapes=[pltpu.VMEM((tm, d), BF16)],
        compiler_params=pltpu.CompilerParams(
            dimension_semantics=("arbitrary", "arbitrary"),
            vmem_limit_bytes=48 * MIB),
        name="in_proj",
    )(x, gain, w)


def _conv_taps(cx_m2, cx_m1, cx, w_ref):
    return w_ref[0:1, :] * cx_m2 + w_ref[1:2, :] * cx_m1 + w_ref[2:3, :] * cx


def _conv_kernel(b_ref, c_ref, x_ref, ch_ref, xh_ref, w_ref, o_ref, *, tiles_per_seq):
    cx = c_ref[...].astype(F32) * x_ref[...].astype(F32)
    y = _conv_taps(pltpu.roll(cx, 2, axis=0), pltpu.roll(cx, 1, axis=0), cx, w_ref)
    o_ref[...] = (b_ref[...].astype(F32) * y).astype(o_ref.dtype)

    h = BF16_ROWS
    at_seq_start = pl.program_id(0) % tiles_per_seq == 0
    halo = ch_ref[...].astype(F32) * xh_ref[...].astype(F32)
    halo = jnp.where(at_seq_start, 0.0, halo)
    row = lax.broadcasted_iota(jnp.int32, (h, 1), 0)
    top = cx[:h]
    m1 = jnp.where(row < 1, pltpu.roll(halo, 1, axis=0), pltpu.roll(top, 1, axis=0))
    m2 = jnp.where(row < 2, pltpu.roll(halo, 2, axis=0), pltpu.roll(top, 2, axis=0))
    y_top = _conv_taps(m2, m1, top, w_ref)
    o_ref[0:h, :] = (b_ref[0:h, :].astype(F32) * y_top).astype(o_ref.dtype)


def _conv(proj, conv_w, seq, *, tm=512):
    n_tok = proj.shape[0]
    cw = conv_w.shape[1]
    halo_blocks = tm // BF16_ROWS

    def halo_map(sec):
        return lambda i: (jnp.maximum(i * halo_blocks - 1, 0), sec)

    return pl.pallas_call(
        functools.partial(_conv_kernel, tiles_per_seq=seq // tm),
        grid=(n_tok // tm,),
        in_specs=[
            pl.BlockSpec((tm, cw), lambda i: (i, 0)),
            pl.BlockSpec((tm, cw), lambda i: (i, 1)),
            pl.BlockSpec((tm, cw), lambda i: (i, 2)),
            pl.BlockSpec((BF16_ROWS, cw), halo_map(1)),
            pl.BlockSpec((BF16_ROWS, cw), halo_map(2)),
            pl.BlockSpec((CONV_K, cw), lambda i: (0, 0)),
        ],
        out_specs=pl.BlockSpec((tm, cw), lambda i: (i, 0)),
        out_shape=jax.ShapeDtypeStruct((n_tok, cw), BF16),
        compiler_params=pltpu.CompilerParams(dimension_semantics=("arbitrary",)),
        name="conv",
    )(proj, proj, proj, proj, proj, conv_w)


def _attn_kernel(lq1_ref, lk1_ref, lq2_ref, lk2_ref, sw_ref, q_ref, k_ref, v_ref, o_ref,
                 q2_ref, sa_ref, sb_ref, m_ref, l_ref, acc_ref, *, tq, rc, lambda_init):
    seq = q_ref.shape[0]
    nq = seq // tq
    n_steps = nq * (nq + 1) // 2
    scale = 1.0 / math.sqrt(DK)
    contract_last = (((1,), (1,)), ((), ()))

    def block(idx):
        return pl.ds(pl.multiple_of(idx * tq, tq), tq)

    def prep_q(i):
        q = q_ref[block(i), :]
        lane = lax.broadcasted_iota(jnp.int32, q.shape, 1)
        qs = (q.astype(F32) * scale).astype(BF16)
        zero = jnp.zeros_like(qs)
        q2_ref[i % 2, 0:tq, :] = jnp.where(lane < DK, qs, zero)
        q2_ref[i % 2, tq:2 * tq, :] = jnp.where(lane >= DK, qs, zero)

    def scores_chunk(q_slot, k, s_ref, r0):
        s_ref[r0:r0 + rc, :] = lax.dot_general(
            q2_ref[q_slot, r0:r0 + rc, :], k, contract_last, preferred_element_type=F32)

    def update_chunk(s_ref, v_ones, r0, masked):
        s = s_ref[r0:r0 + rc, :]
        if masked:
            q_pos = lax.broadcasted_iota(jnp.int32, (rc, tq), 0) + (r0 % tq)
            k_pos = lax.broadcasted_iota(jnp.int32, (rc, tq), 1)
            s = jnp.where(k_pos <= q_pos, s, -jnp.inf)
        cols = [s[:, c0:c0 + LANES] for c0 in range(0, tq, LANES)]
        lane_max = functools.reduce(jnp.maximum, cols)
        row_max = jnp.broadcast_to(jnp.max(lane_max, axis=-1, keepdims=True), (rc, LANES))
        rows = pl.ds(r0, rc)
        m_prev = m_ref[rows, :]
        m_new = jnp.maximum(m_prev, row_max)
        alpha = jnp.exp(m_prev - m_new)
        p = jnp.concatenate([jnp.exp(col - m_new) for col in cols], axis=1).astype(BF16)
        pv = jnp.dot(p, v_ones, preferred_element_type=F32)
        acc_ref[rows, :] = alpha * acc_ref[rows, :] + pv[:, :DV]
        l_ref[rows, :] = alpha * l_ref[rows, :] + pv[:, DV:]
        m_ref[rows, :] = m_new

    def step(j, q_slot_next, j_next, s_cur, s_next, masked):
        k_next = k_ref[block(j_next), :]
        v = v_ref[block(j), :]
        v_ones = jnp.concatenate([v, jnp.ones_like(v)], axis=1)
        for r0 in range(0, 2 * tq, rc):
            scores_chunk(q_slot_next, k_next, s_next, r0)
            update_chunk(s_cur, v_ones, r0, masked)

    def finish(i):
        lam = (jnp.exp(jnp.sum(lq1_ref[...] * lk1_ref[...], axis=-1, keepdims=True))
               - jnp.exp(jnp.sum(lq2_ref[...] * lk2_ref[...], axis=-1, keepdims=True))
               + lambda_init)
        o_all = acc_ref[...] / l_ref[...]
        o = o_all[0:tq] - lam * o_all[tq:2 * tq]
        o_ref[block(i), :] = (_rms(o, sw_ref[...]) * (1.0 - lambda_init)).astype(o_ref.dtype)

    def advance(i, j):
        diag = j == i
        return jnp.where(diag, i + 1, i), jnp.where(diag, 0, j + 1)

    def start_q_block(i, j, valid):
        first = jnp.logical_and(valid, j == 0)

        @pl.when(first)
        def _():
            m_ref[...] = jnp.full_like(m_ref, -jnp.inf)
            l_ref[...] = jnp.zeros_like(l_ref)
            acc_ref[...] = jnp.zeros_like(acc_ref)

        @pl.when(jnp.logical_and(first, i + 1 < nq))
        def _():
            prep_q(i + 1)

    def one_pair(i, j, valid, s_cur, s_next):
        diag = j == i
        i_next, j_next = advance(i, j)
        q_slot_next = jnp.minimum(i_next, nq - 1) % 2

        @pl.when(jnp.logical_and(valid, diag))
        def _():
            step(j, q_slot_next, j_next, s_cur, s_next, masked=True)
            finish(i)

        @pl.when(jnp.logical_and(valid, jnp.logical_not(diag)))
        def _():
            step(j, q_slot_next, j_next, s_cur, s_next, masked=False)

    def body(t2, carry):
        i, j = carry
        i1, j1 = advance(i, j)
        second_valid = 2 * t2 + 1 < n_steps
        straight = jnp.logical_and(second_valid, j + 1 < i)
        start_q_block(i, j, True)

        @pl.when(straight)
        def _():
            step(j, i % 2, j + 1, sa_ref, sb_ref, masked=False)
            step(j + 1, i % 2, j + 2, sb_ref, sa_ref, masked=False)

        @pl.when(jnp.logical_not(straight))
        def _():
            one_pair(i, j, True, sa_ref, sb_ref)
            start_q_block(i1, j1, second_valid)
            one_pair(i1, j1, second_valid, sb_ref, sa_ref)

        return advance(i1, j1)

    prep_q(0)
    k0 = k_ref[block(0), :]
    for r0 in range(0, 2 * tq, rc):
        scores_chunk(0, k0, sa_ref, r0)
    lax.fori_loop(0, (n_steps + 1) // 2, body, (jnp.int32(0), jnp.int32(0)))


def _attn(proj, lq1, lk1, lq2, lk2, subln, batch, seq, lambda_init, *, tq=512, rc=128):
    n_tok = proj.shape[0]
    q_blk = 3 * (N_HEADS * DV) // LANES
    k_blk = q_blk + N_HEADS
    v_blk = k_blk + N_HEADS
    vec = lambda width: pl.BlockSpec((1, width), lambda b, h: (0, 0))
    head = lambda first_blk: pl.BlockSpec((seq, LANES), lambda b, h: (b, first_blk + h))
    return pl.pallas_call(
        functools.partial(_attn_kernel, tq=tq, rc=rc, lambda_init=lambda_init),
        grid=(batch, N_HEADS),
        in_specs=[vec(DK), vec(DK), vec(DK), vec(DK), vec(DV), head(q_blk), head(k_blk), head(v_blk)],
        out_specs=head(0),
        out_shape=jax.ShapeDtypeStruct((n_tok, N_HEADS * DV), BF16),
        scratch_shapes=[
            pltpu.VMEM((2, 2 * tq, LANES), BF16),
            pltpu.VMEM((2 * tq, tq), F32),
            pltpu.VMEM((2 * tq, tq), F32),
            pltpu.VMEM((2 * tq, LANES), F32),
            pltpu.VMEM((2 * tq, LANES), F32),
            pltpu.VMEM((2 * tq, DV), F32),
        ],
        compiler_params=pltpu.CompilerParams(
            dimension_semantics=("arbitrary", "arbitrary"),
            vmem_limit_bytes=48 * MIB),
        name="diff_attn",
    )(lq1, lk1, lq2, lk2, subln, proj, proj, proj)


def _out_proj_kernel(x_ref, yc_ref, ya_ref, wc_ref, wa_ref, o_ref):
    y = jnp.dot(yc_ref[...], wc_ref[...], preferred_element_type=F32)
    y += jnp.dot(ya_ref[...], wa_ref[...], preferred_element_type=F32)
    o_ref[...] = x_ref[...] + y


def _out_proj(x, y_conv, y_attn, w_out, *, tm=512):
    n_tok, d = x.shape
    cw = y_conv.shape[1]
    aw = y_attn.shape[1]
    return pl.pallas_call(
        _out_proj_kernel,
        grid=(n_tok // tm,),
        in_specs=[
            pl.BlockSpec((tm, d), lambda i: (i, 0)),
            pl.BlockSpec((tm, cw), lambda i: (i, 0)),
            pl.BlockSpec((tm, aw), lambda i: (i, 0)),
            pl.BlockSpec((cw, d), lambda i: (0, 0)),
            pl.BlockSpec((aw, d), lambda i: (1, 0)),
        ],
        out_specs=pl.BlockSpec((tm, d), lambda i: (i, 0)),
        out_shape=jax.ShapeDtypeStruct((n_tok, d), F32),
        compiler_params=pltpu.CompilerParams(
            dimension_semantics=("arbitrary",),
            vmem_limit_bytes=48 * MIB),
        name="out_proj",
    )(x, y_conv, y_attn, w_out, w_out)


def kernel(x, ffn1_norm, ffn1_w_gate, ffn1_w_up, ffn1_w_down, mix_norm, w_in, conv_w, lambda_q1, lambda_k1, lambda_q2, lambda_k2, subln_w, w_out, ffn2_norm, ffn2_w_gate, ffn2_w_up, ffn2_w_down, final_norm):
    batch, seq, d = x.shape
    depth = ffn1_norm.shape[0]
    assert w_in.shape[2] == 3 * conv_w.shape[2] + 3 * N_HEADS * DV
    assert conv_w.shape[2] == N_HEADS * DV
    h = x.reshape(batch * seq, d)
    row = lambda a: a.reshape(1, -1)
    for l in range(depth):
        last = l == depth - 1
        lambda_init = 0.8 - 0.6 * math.exp(-0.3 * l)
        h = _ffn(h, row(ffn1_norm[l]), ffn1_w_gate[l].astype(BF16), ffn1_w_up[l].astype(BF16),
                 ffn1_w_down[l].astype(BF16), row(final_norm), final_norm=False)
        proj = _in_proj(h, row(mix_norm[l]), w_in[l].astype(BF16))
        y_conv = _conv(proj, conv_w[l], seq)
        y_attn = _attn(proj, row(lambda_q1[l]), row(lambda_k1[l]), row(lambda_q2[l]), row(lambda_k2[l]),
                       row(subln_w[l]), batch, seq, lambda_init)
        h = _out_proj(h, y_conv, y_attn, w_out[l].astype(BF16))
        h = _ffn(h, row(ffn2_norm[l]), ffn2_w_gate[l].astype(BF16), ffn2_w_up[l].astype(BF16),
                 ffn2_w_down[l].astype(BF16), row(final_norm), final_norm=last)
    if depth == 0:
        raise ValueError("depth must be at least 1")
    return h.reshape(batch, seq, d)
```

```python
import functools
import math

import jax
import jax.numpy as jnp
from jax import lax
from jax.experimental import pallas as pl
from jax.experimental.pallas import tpu as pltpu

EPS = 1e-6
CONV_K = 3
N_HEADS = 8
DK = 64
DV = 128
LANES = 128
BF16_ROWS = 16
ROW_CHUNK = 128
MIB = 1024 * 1024

F32 = jnp.float32
BF16 = jnp.bfloat16


def _rms(x, g):
    return x * lax.rsqrt(jnp.mean(x * x, axis=-1, keepdims=True) + EPS) * g


def _for_row_chunks(n_rows, fn):
    def chunk(c, carry):
        fn(pl.ds(pl.multiple_of(c * ROW_CHUNK, ROW_CHUNK), ROW_CHUNK))
        return carry
    lax.fori_loop(0, n_rows // ROW_CHUNK, chunk, 0)


def _ffn_kernel(x_ref, g_ref, wg_ref, wu_ref, wd_ref, fg_ref, o_ref, n_ref, *, final_norm):
    f = pl.program_id(1)
    for_row_chunks = functools.partial(_for_row_chunks, x_ref.shape[0])

    @pl.when(f == 0)
    def _():
        def norm_rows(rows):
            n_ref[rows, :] = _rms(x_ref[rows, :], g_ref[...]).astype(BF16)
            o_ref[rows, :] = jnp.zeros((ROW_CHUNK, o_ref.shape[1]), F32)
        for_row_chunks(norm_rows)

    n = n_ref[...]
    gate = jnp.dot(n, wg_ref[...], preferred_element_type=F32)
    up = jnp.dot(n, wu_ref[...], preferred_element_type=F32)
    h = (gate * jax.nn.sigmoid(gate) * up).astype(BF16)
    o_ref[...] += jnp.dot(h, wd_ref[...], preferred_element_type=F32)

    @pl.when(f == pl.num_programs(1) - 1)
    def _():
        def residual_rows(rows):
            y = x_ref[rows, :] + 0.5 * o_ref[rows, :]
            if final_norm:
                y = _rms(y, fg_ref[...])
            o_ref[rows, :] = y
        for_row_chunks(residual_rows)


def _ffn(x, gain, wg, wu, wd, final_gain, *, final_norm, tm=1024, tf=512):
    n_tok, d = x.shape
    ffn = wg.shape[1]
    return pl.pallas_call(
        functools.partial(_ffn_kernel, final_norm=final_norm),
        grid=(n_tok // tm, ffn // tf),
        in_specs=[
            pl.BlockSpec((tm, d), lambda i, f: (i, 0)),
            pl.BlockSpec((1, d), lambda i, f: (0, 0)),
            pl.BlockSpec((d, tf), lambda i, f: (0, f)),
            pl.BlockSpec((d, tf), lambda i, f: (0, f)),
            pl.BlockSpec((tf, d), lambda i, f: (f, 0)),
            pl.BlockSpec((1, d), lambda i, f: (0, 0)),
        ],
        out_specs=pl.BlockSpec((tm, d), lambda i, f: (i, 0)),
        out_shape=jax.ShapeDtypeStruct((n_tok, d), F32),
        scratch_shapes=[pltpu.VMEM((tm, d), BF16)],
        compiler_params=pltpu.CompilerParams(
            dimension_semantics=("arbitrary", "arbitrary"),
            vmem_limit_bytes=56 * MIB),
        name="ffn",
    )(x, gain, wg, wu, wd, final_gain)


def _in_proj_kernel(x_ref, g_ref, w_ref, o_ref, n_ref):
    @pl.when(pl.program_id(1) == 0)
    def _():
        def norm_rows(rows):
            n_ref[rows, :] = _rms(x_ref[rows, :], g_ref[...]).astype(BF16)
        _for_row_chunks(x_ref.shape[0], norm_rows)

    o_ref[...] = jnp.dot(n_ref[...], w_ref[...], preferred_element_type=F32).astype(o_ref.dtype)


def _in_proj(x, gain, w, *, tm=1024, tn=2048):
    n_tok, d = x.shape
    cols = w.shape[1]
    return pl.pallas_call(
        _in_proj_kernel,
        grid=(n_tok // tm, cols // tn),
        in_specs=[
            pl.BlockSpec((tm, d), lambda i, j: (i, 0)),
            pl.BlockSpec((1, d), lambda i, j: (0, 0)),
            pl.BlockSpec((d, tn), lambda i, j: (0, j)),
        ],
        out_specs=pl.BlockSpec((tm, tn), lambda i, j: (i, j)),
        out_shape=jax.ShapeDtypeStruct((n_tok, cols), BF16),
        scratch_shapes=[pltpu.VMEM((tm, d), BF16)],
        compiler_params=pltpu.CompilerParams(
            dimension_semantics=("arbitrary", "arbitrary"),
            vmem_limit_bytes=48 * MIB),
        name="in_proj",
    )(x, gain, w)


def _conv_taps(cx_m2, cx_m1, cx, w_ref):
    return w_ref[0:1, :] * cx_m2 + w_ref[1:2, :] * cx_m1 + w_ref[2:3, :] * cx


def _conv_kernel(b_ref, c_ref, x_ref, ch_ref, xh_ref, w_ref, o_ref, *, tiles_per_seq):
    cx = c_ref[...].astype(F32) * x_ref[...].astype(F32)
    y = _conv_taps(pltpu.roll(cx, 2, axis=0), pltpu.roll(cx, 1, axis=0), cx, w_ref)
    o_ref[...] = (b_ref[...].astype(F32) * y).astype(o_ref.dtype)

    h = BF16_ROWS
    at_seq_start = pl.program_id(0) % tiles_per_seq == 0
    halo = ch_ref[...].astype(F32) * xh_ref[...].astype(F32)
    halo = jnp.where(at_seq_start, 0.0, halo)
    row = lax.broadcasted_iota(jnp.int32, (h, 1), 0)
    top = cx[:h]
    m1 = jnp.where(row < 1, pltpu.roll(halo, 1, axis=0), pltpu.roll(top, 1, axis=0))
    m2 = jnp.where(row < 2, pltpu.roll(halo, 2, axis=0), pltpu.roll(top, 2, axis=0))
    y_top = _conv_taps(m2, m1, top, w_ref)
    o_ref[0:h, :] = (b_ref[0:h, :].astype(F32) * y_top).astype(o_ref.dtype)


def _conv(proj, conv_w, seq, *, tm=512):
    n_tok = proj.shape[0]
    cw = conv_w.shape[1]
    halo_blocks = tm // BF16_ROWS

    def halo_map(sec):
        return lambda i: (jnp.maximum(i * halo_blocks - 1, 0), sec)

    return pl.pallas_call(
        functools.partial(_conv_kernel, tiles_per_seq=seq // tm),
        grid=(n_tok // tm,),
        in_specs=[
            pl.BlockSpec((tm, cw), lambda i: (i, 0)),
            pl.BlockSpec((tm, cw), lambda i: (i, 1)),
            pl.BlockSpec((tm, cw), lambda i: (i, 2)),
            pl.BlockSpec((BF16_ROWS, cw), halo_map(1)),
            pl.BlockSpec((BF16_ROWS, cw), halo_map(2)),
            pl.BlockSpec((CONV_K, cw), lambda i: (0, 0)),
        ],
        out_specs=pl.BlockSpec((tm, cw), lambda i: (i, 0)),
        out_shape=jax.ShapeDtypeStruct((n_tok, cw), BF16),
        compiler_params=pltpu.CompilerParams(dimension_semantics=("arbitrary",)),
        name="conv",
    )(proj, proj, proj, proj, proj, conv_w)


def _attn_kernel(lq1_ref, lk1_ref, lq2_ref, lk2_ref, sw_ref, q_ref, k_ref, v_ref, o_ref,
                 q2_ref, sa_ref, sb_ref, m_ref, l_ref, acc_ref, *, tq, rc, lambda_init):
    seq = q_ref.shape[0]
    nq = seq // tq
    n_steps = nq * (nq + 1) // 2
    scale = 1.0 / math.sqrt(DK)
    contract_last = (((1,), (1,)), ((), ()))

    def block(idx):
        return pl.ds(pl.multiple_of(idx * tq, tq), tq)

    def prep_q(i):
        q = q_ref[block(i), :]
        lane = lax.broadcasted_iota(jnp.int32, q.shape, 1)
        qs = (q.astype(F32) * scale).astype(BF16)
        zero = jnp.zeros_like(qs)
        q2_ref[i % 2, 0:tq, :] = jnp.where(lane < DK, qs, zero)
        q2_ref[i % 2, tq:2 * tq, :] = jnp.where(lane >= DK, qs, zero)

    def scores_chunk(q_slot, k, s_ref, r0):
        s_ref[r0:r0 + rc, :] = lax.dot_general(
            q2_ref[q_slot, r0:r0 + rc, :], k, contract_last, preferred_element_type=F32)

    def update_chunk(s_ref, v_ones, r0, masked):
        s = s_ref[r0:r0 + rc, :]
        if masked:
            q_pos = lax.broadcasted_iota(jnp.int32, (rc, tq), 0) + (r0 % tq)
            k_pos = lax.broadcasted_iota(jnp.int32, (rc, tq), 1)
            s = jnp.where(k_pos <= q_pos, s, -jnp.inf)
        cols = [s[:, c0:c0 + LANES] for c0 in range(0, tq, LANES)]
        lane_max = functools.reduce(jnp.maximum, cols)
        row_max = jnp.broadcast_to(jnp.max(lane_max, axis=-1, keepdims=True), (rc, LANES))
        rows = pl.ds(r0, rc)
        m_prev = m_ref[rows, :]
        m_new = jnp.maximum(m_prev, row_max)
        alpha = jnp.exp(m_prev - m_new)
        p = jnp.concatenate([jnp.exp(col - m_new) for col in cols], axis=1).astype(BF16)
        pv = jnp.dot(p, v_ones, preferred_element_type=F32)
        acc_ref[rows, :] = alpha * acc_ref[rows, :] + pv[:, :DV]
        l_ref[rows, :] = alpha * l_ref[rows, :] + pv[:, DV:]
        m_ref[rows, :] = m_new

    def step(j, q_slot_next, j_next, s_cur, s_next, masked):
        k_next = k_ref[block(j_next), :]
        v = v_ref[block(j), :]
        v_ones = jnp.concatenate([v, jnp.ones_like(v)], axis=1)
        for r0 in range(0, 2 * tq, rc):
            scores_chunk(q_slot_next, k_next, s_next, r0)
            update_chunk(s_cur, v_ones, r0, masked)

    def finish(i):
        lam = (jnp.exp(jnp.sum(lq1_ref[...] * lk1_ref[...], axis=-1, keepdims=True))
               - jnp.exp(jnp.sum(lq2_ref[...] * lk2_ref[...], axis=-1, keepdims=True))
               + lambda_init)
        o_all = acc_ref[...] / l_ref[...]
        o = o_all[0:tq] - lam * o_all[tq:2 * tq]
        o_ref[block(i), :] = (_rms(o, sw_ref[...]) * (1.0 - lambda_init)).astype(o_ref.dtype)

    def advance(i, j):
        diag = j == i
        return jnp.where(diag, i + 1, i), jnp.where(diag, 0, j + 1)

    def start_q_block(i, j, valid):
        first = jnp.logical_and(valid, j == 0)

        @pl.when(first)
        def _():
            m_ref[...] = jnp.full_like(m_ref, -jnp.inf)
            l_ref[...] = jnp.zeros_like(l_ref)
            acc_ref[...] = jnp.zeros_like(acc_ref)

        @pl.when(jnp.logical_and(first, i + 1 < nq))
        def _():
            prep_q(i + 1)

    def one_pair(i, j, valid, s_cur, s_next):
        diag = j == i
        i_next, j_next = advance(i, j)
        q_slot_next = jnp.minimum(i_next, nq - 1) % 2

        @pl.when(jnp.logical_and(valid, diag))
        def _():
            step(j, q_slot_next, j_next, s_cur, s_next, masked=True)
            finish(i)

        @pl.when(jnp.logical_and(valid, jnp.logical_not(diag)))
        def _():
            step(j, q_slot_next, j_next, s_cur, s_next, masked=False)

    def body(t2, carry):
        i, j = carry
        i1, j1 = advance(i, j)
        second_valid = 2 * t2 + 1 < n_steps
        straight = jnp.logical_and(second_valid, j + 1 < i)
        start_q_block(i, j, True)

        @pl.when(straight)
        def _():
            step(j, i % 2, j + 1, sa_ref, sb_ref, masked=False)
            step(j + 1, i % 2, j + 2, sb_ref, sa_ref, masked=False)

        @pl.when(jnp.logical_not(straight))
        def _():
            one_pair(i, j, True, sa_ref, sb_ref)
            start_q_block(i1, j1, second_valid)
            one_pair(i1, j1, second_valid, sb_ref, sa_ref)

        return advance(i1, j1)

    prep_q(0)
    k0 = k_ref[block(0), :]
    for r0 in range(0, 2 * tq, rc):
        scores_chunk(0, k0, sa_ref, r0)
    lax.fori_loop(0, (n_steps + 1) // 2, body, (jnp.int32(0), jnp.int32(0)))


def _attn(proj, lq1, lk1, lq2, lk2, subln, batch, seq, lambda_init, *, tq=512, rc=128):
    n_tok = proj.shape[0]
    q_blk = 3 * (N_HEADS * DV) // LANES
    k_blk = q_blk + N_HEADS
    v_blk = k_blk + N_HEADS
    vec = lambda width: pl.BlockSpec((1, width), lambda b, h: (0, 0))
    head = lambda first_blk: pl.BlockSpec((seq, LANES), lambda b, h: (b, first_blk + h))
    return pl.pallas_call(
        functools.partial(_attn_kernel, tq=tq, rc=rc, lambda_init=lambda_init),
        grid=(batch, N_HEADS),
        in_specs=[vec(DK), vec(DK), vec(DK), vec(DK), vec(DV), head(q_blk), head(k_blk), head(v_blk)],
        out_specs=head(0),
        out_shape=jax.ShapeDtypeStruct((n_tok, N_HEADS * DV), BF16),
        scratch_shapes=[
            pltpu.VMEM((2, 2 * tq, LANES), BF16),
            pltpu.VMEM((2 * tq, tq), F32),
            pltpu.VMEM((2 * tq, tq), F32),
            pltpu.VMEM((2 * tq, LANES), F32),
            pltpu.VMEM((2 * tq, LANES), F32),
            pltpu.VMEM((2 * tq, DV), F32),
        ],
        compiler_params=pltpu.CompilerParams(
            dimension_semantics=("arbitrary", "arbitrary"),
            vmem_limit_bytes=48 * MIB),
        name="diff_attn",
    )(lq1, lk1, lq2, lk2, subln, proj, proj, proj)


def _out_proj_kernel(x_ref, yc_ref, ya_ref, wc_ref, wa_ref, o_ref):
    y = jnp.dot(yc_ref[...], wc_ref[...], preferred_element_type=F32)
    y += jnp.dot(ya_ref[...], wa_ref[...], preferred_element_type=F32)
    o_ref[...] = x_ref[...] + y


def _out_proj(x, y_conv, y_attn, w_out, *, tm=512):
    n_tok, d = x.shape
    cw = y_conv.shape[1]
    aw = y_attn.shape[1]
    return pl.pallas_call(
        _out_proj_kernel,
        grid=(n_tok // tm,),
        in_specs=[
            pl.BlockSpec((tm, d), lambda i: (i, 0)),
            pl.BlockSpec((tm, cw), lambda i: (i, 0)),
            pl.BlockSpec((tm, aw), lambda i: (i, 0)),
            pl.BlockSpec((cw, d), lambda i: (0, 0)),
            pl.BlockSpec((aw, d), lambda i: (1, 0)),
        ],
        out_specs=pl.BlockSpec((tm, d), lambda i: (i, 0)),
        out_shape=jax.ShapeDtypeStruct((n_tok, d), F32),
        compiler_params=pltpu.CompilerParams(
            dimension_semantics=("arbitrary",),
            vmem_limit_bytes=48 * MIB),
        name="out_proj",
    )(x, y_conv, y_attn, w_out, w_out)


def kernel(x, ffn1_norm, ffn1_w_gate, ffn1_w_up, ffn1_w_down, mix_norm, w_in, conv_w, lambda_q1, lambda_k1, lambda_q2, lambda_k2, subln_w, w_out, ffn2_norm, ffn2_w_gate, ffn2_w_up, ffn2_w_down, final_norm):
    batch, seq, d = x.shape
    depth = ffn1_norm.shape[0]
    assert w_in.shape[2] == 3 * conv_w.shape[2] + 3 * N_HEADS * DV
    assert conv_w.shape[2] == N_HEADS * DV
    h = x.reshape(batch * seq, d)
    row = lambda a: a.reshape(1, -1)
    for l in range(depth):
        last = l == depth - 1
        lambda_init = 0.8 - 0.6 * math.exp(-0.3 * l)
        h = _ffn(h, row(ffn1_norm[l]), ffn1_w_gate[l].astype(BF16), ffn1_w_up[l].astype(BF16),
                 ffn1_w_down[l].astype(BF16), row(final_norm), final_norm=False)
        proj = _in_proj(h, row(mix_norm[l]), w_in[l].astype(BF16))
        y_conv = _conv(proj, conv_w[l], seq)
        y_attn = _attn(proj, row(lambda_q1[l]), row(lambda_k1[l]), row(lambda_q2[l]), row(lambda_k2[l]),
                       row(subln_w[l]), batch, seq, lambda_init)
        h = _out_proj(h, y_conv, y_attn, w_out[l].astype(BF16))
        h = _ffn(h, row(ffn2_norm[l]), ffn2_w_gate[l].astype(BF16), ffn2_w_up[l].astype(BF16),
                 ffn2_w_down[l].astype(BF16), row(final_norm), final_norm=last)
    if depth == 0:
        raise ValueError("depth must be at least 1")
    return h.reshape(batch, seq, d)
```

```python
import functools
import math

import jax
import jax.numpy as jnp
from jax import lax
from jax.experimental import pallas as pl
from jax.experimental.pallas import tpu as pltpu

EPS = 1e-6
CONV_K = 3
N_HEADS = 8
DK = 64
DV = 128
LANES = 128
BF16_ROWS = 16
ROW_CHUNK = 128
MIB = 1024 * 1024

F32 = jnp.float32
BF16 = jnp.bfloat16


def _rms(x, g):
    return x * lax.rsqrt(jnp.mean(x * x, axis=-1, keepdims=True) + EPS) * g


def _for_row_chunks(n_rows, fn):
    def chunk(c, carry):
        fn(pl.ds(pl.multiple_of(c * ROW_CHUNK, ROW_CHUNK), ROW_CHUNK))
        return carry
    lax.fori_loop(0, n_rows // ROW_CHUNK, chunk, 0)


def _ffn_kernel(x_ref, g_ref, wg_ref, wu_ref, wd_ref, fg_ref, o_ref, n_ref, *, final_norm):
    f = pl.program_id(1)
    for_row_chunks = functools.partial(_for_row_chunks, x_ref.shape[0])

    @pl.when(f == 0)
    def _():
        def norm_rows(rows):
            n_ref[rows, :] = _rms(x_ref[rows, :], g_ref[...]).astype(BF16)
            o_ref[rows, :] = jnp.zeros((ROW_CHUNK, o_ref.shape[1]), F32)
        for_row_chunks(norm_rows)

    n = n_ref[...]
    gate = jnp.dot(n, wg_ref[...], preferred_element_type=F32)
    up = jnp.dot(n, wu_ref[...], preferred_element_type=F32)
    h = (gate * jax.nn.sigmoid(gate) * up).astype(BF16)
    o_ref[...] += jnp.dot(h, wd_ref[...], preferred_element_type=F32)

    @pl.when(f == pl.num_programs(1) - 1)
    def _():
        def residual_rows(rows):
            y = x_ref[rows, :] + 0.5 * o_ref[rows, :]
            if final_norm:
                y = _rms(y, fg_ref[...])
            o_ref[rows, :] = y
        for_row_chunks(residual_rows)


def _ffn(x, gain, wg, wu, wd, final_gain, *, final_norm, tm=1024, tf=512):
    n_tok, d = x.shape
    ffn = wg.shape[1]
    return pl.pallas_call(
        functools.partial(_ffn_kernel, final_norm=final_norm),
        grid=(n_tok // tm, ffn // tf),
        in_specs=[
            pl.BlockSpec((tm, d), lambda i, f: (i, 0)),
            pl.BlockSpec((1, d), lambda i, f: (0, 0)),
            pl.BlockSpec((d, tf), lambda i, f: (0, f)),
            pl.BlockSpec((d, tf), lambda i, f: (0, f)),
            pl.BlockSpec((tf, d), lambda i, f: (f, 0)),
            pl.BlockSpec((1, d), lambda i, f: (0, 0)),
        ],
        out_specs=pl.BlockSpec((tm, d), lambda i, f: (i, 0)),
        out_shape=jax.ShapeDtypeStruct((n_tok, d), F32),
        scratch_shapes=[pltpu.VMEM((tm, d), BF16)],
        compiler_params=pltpu.CompilerParams(
            dimension_semantics=("arbitrary", "arbitrary"),
            vmem_limit_bytes=56 * MIB),
        name="ffn",
    )(x, gain, wg, wu, wd, final_gain)


def _in_proj_kernel(x_ref, g_ref, w_ref, o_ref, n_ref):
    @pl.when(pl.program_id(1) == 0)
    def _():
        def norm_rows(rows):
            n_ref[rows, :] = _rms(x_ref[rows, :], g_ref[...]).astype(BF16)
        _for_row_chunks(x_ref.shape[0], norm_rows)

    o_ref[...] = jnp.dot(n_ref[...], w_ref[...], preferred_element_type=F32).astype(o_ref.dtype)


def _in_proj(x, gain, w, *, tm=1024, tn=2048):
    n_tok, d = x.shape
    cols = w.shape[1]
    return pl.pallas_call(
        _in_proj_kernel,
        grid=(n_tok // tm, cols // tn),
        in_specs=[
            pl.BlockSpec((tm, d), lambda i, j: (i, 0)),
            pl.BlockSpec((1, d), lambda i, j: (0, 0)),
            pl.BlockSpec((d, tn), lambda i, j: (0, j)),
        ],
        out_specs=pl.BlockSpec((tm, tn), lambda i, j: (i, j)),
        out_shape=jax.ShapeDtypeStruct((n_tok, cols), BF16),
        scratch_shapes=[pltpu.VMEM((tm, d), BF16)],
        compiler_params=pltpu.CompilerParams(
            dimension_semantics=("arbitrary", "arbitrary"),
            vmem_limit_bytes=48 * MIB),
        name="in_proj",
    )(x, gain, w)


def _conv_taps(cx_m2, cx_m1, cx, w_ref):
    return w_ref[0:1, :] * cx_m2 + w_ref[1:2, :] * cx_m1 + w_ref[2:3, :] * cx


def _conv_kernel(b_ref, c_ref, x_ref, ch_ref, xh_ref, w_ref, o_ref, *, tiles_per_seq):
    cx = c_ref[...].astype(F32) * x_ref[...].astype(F32)
    y = _conv_taps(pltpu.roll(cx, 2, axis=0), pltpu.roll(cx, 1, axis=0), cx, w_ref)
    o_ref[...] = (b_ref[...].astype(F32) * y).astype(o_ref.dtype)

    h = BF16_ROWS
    at_seq_start = pl.program_id(0) % tiles_per_seq == 0
    halo = ch_ref[...].astype(F32) * xh_ref[...].astype(F32)
    halo = jnp.where(at_seq_start, 0.0, halo)
    row = lax.broadcasted_iota(jnp.int32, (h, 1), 0)
    top = cx[:h]
    m1 = jnp.where(row < 1, pltpu.roll(halo, 1, axis=0), pltpu.roll(top, 1, axis=0))
    m2 = jnp.where(row < 2, pltpu.roll(halo, 2, axis=0), pltpu.roll(top, 2, axis=0))
    y_top = _conv_taps(m2, m1, top, w_ref)
    o_ref[0:h, :] = (b_ref[0:h, :].astype(F32) * y_top).astype(o_ref.dtype)


def _conv(proj, conv_w, seq, *, tm=512):
    n_tok = proj.shape[0]
    cw = conv_w.shape[1]
    halo_blocks = tm // BF16_ROWS

    def halo_map(sec):
        return lambda i: (jnp.maximum(i * halo_blocks - 1, 0), sec)

    return pl.pallas_call(
        functools.partial(_conv_kernel, tiles_per_seq=seq // tm),
        grid=(n_tok // tm,),
        in_specs=[
            pl.BlockSpec((tm, cw), lambda i: (i, 0)),
            pl.BlockSpec((tm, cw), lambda i: (i, 1)),
            pl.BlockSpec((tm, cw), lambda i: (i, 2)),
            pl.BlockSpec((BF16_ROWS, cw), halo_map(1)),
            pl.BlockSpec((BF16_ROWS, cw), halo_map(2)),
            pl.BlockSpec((CONV_K, cw), lambda i: (0, 0)),
        ],
        out_specs=pl.BlockSpec((tm, cw), lambda i: (i, 0)),
        out_shape=jax.ShapeDtypeStruct((n_tok, cw), BF16),
        compiler_params=pltpu.CompilerParams(dimension_semantics=("arbitrary",)),
        name="conv",
    )(proj, proj, proj, proj, proj, conv_w)


def _attn_kernel(lq1_ref, lk1_ref, lq2_ref, lk2_ref, sw_ref, q_ref, k_ref, v_ref, o_ref,
                 q2_ref, sa_ref, sb_ref, m_ref, l_ref, acc_ref, *, tq, rc, lambda_init):
    seq = q_ref.shape[0]
    nq = seq // tq
    scale = 1.0 / math.sqrt(DK)
    contract_last = (((1,), (1,)), ((), ()))

    def block(idx):
        return pl.ds(pl.multiple_of(idx * tq, tq), tq)

    def prep_q(i):
        q = q_ref[block(i), :]
        lane = lax.broadcasted_iota(jnp.int32, q.shape, 1)
        qs = (q.astype(F32) * scale).astype(BF16)
        zero = jnp.zeros_like(qs)
        q2_ref[i % 2, 0:tq, :] = jnp.where(lane < DK, qs, zero)
        q2_ref[i % 2, tq:2 * tq, :] = jnp.where(lane >= DK, qs, zero)

    def scores_chunk(q_slot, k, s_ref, r0):
        s_ref[r0:r0 + rc, :] = lax.dot_general(
            q2_ref[q_slot, r0:r0 + rc, :], k, contract_last, preferred_element_type=F32)

    def update_chunk(s_ref, v_ones, r0, masked):
        s = s_ref[r0:r0 + rc, :]
        if masked:
            q_pos = lax.broadcasted_iota(jnp.int32, (rc, tq), 0) + (r0 % tq)
            k_pos = lax.broadcasted_iota(jnp.int32, (rc, tq), 1)
            s = jnp.where(k_pos <= q_pos, s, -jnp.inf)
        cols = [s[:, c0:c0 + LANES] for c0 in range(0, tq, LANES)]
        lane_max = functools.reduce(jnp.maximum, cols)
        row_max = jnp.broadcast_to(jnp.max(lane_max, axis=-1, keepdims=True), (rc, LANES))
        rows = pl.ds(r0, rc)
        m_prev = m_ref[rows, :]
        m_new = jnp.maximum(m_prev, row_max)
        alpha = jnp.exp(m_prev - m_new)
        p = jnp.concatenate([jnp.exp(col - m_new) for col in cols], axis=1).astype(BF16)
        pv = jnp.dot(p, v_ones, preferred_element_type=F32)
        acc_ref[rows, :] = alpha * acc_ref[rows, :] + pv[:, :DV]
        l_ref[rows, :] = alpha * l_ref[rows, :] + pv[:, DV:]
        m_ref[rows, :] = m_new

    def step(j, q_slot_next, j_next, s_cur, s_next, masked):
        k_next = k_ref[block(j_next), :]
        v = v_ref[block(j), :]
        v_ones = jnp.concatenate([v, jnp.ones_like(v)], axis=1)
        for r0 in range(0, 2 * tq, rc):
            if s_next is s_cur:
                update_chunk(s_cur, v_ones, r0, masked)
                scores_chunk(q_slot_next, k_next, s_next, r0)
            else:
                scores_chunk(q_slot_next, k_next, s_next, r0)
                update_chunk(s_cur, v_ones, r0, masked)

    def finish(i):
        lam = (jnp.exp(jnp.sum(lq1_ref[...] * lk1_ref[...], axis=-1, keepdims=True))
               - jnp.exp(jnp.sum(lq2_ref[...] * lk2_ref[...], axis=-1, keepdims=True))
               + lambda_init)
        o_all = acc_ref[...] / l_ref[...]
        o = o_all[0:tq] - lam * o_all[tq:2 * tq]
        o_ref[block(i), :] = (_rms(o, sw_ref[...]) * (1.0 - lambda_init)).astype(o_ref.dtype)

    def straight_steps(i, j, count):
        for u in range(0, count, 2):
            step(j + u, i % 2, j + u + 1, sa_ref, sb_ref, masked=False)
            step(j + u + 1, i % 2, j + u + 2, sb_ref, sa_ref, masked=False)

    def q_block(i, carry):
        m_ref[...] = jnp.full_like(m_ref, -jnp.inf)
        l_ref[...] = jnp.zeros_like(l_ref)
        acc_ref[...] = jnp.zeros_like(acc_ref)

        @pl.when(i + 1 < nq)
        def _():
            prep_q(i + 1)

        def quad(g, c):
            straight_steps(i, 4 * g, 4)
            return c

        n_quads = i // 4
        lax.fori_loop(0, n_quads, quad, 0)
        rest = i - 4 * n_quads
        q_slot_after = jnp.minimum(i + 1, nq - 1) % 2

        @pl.when(rest >= 2)
        def _():
            straight_steps(i, 4 * n_quads, 2)

        @pl.when(rest % 2 == 1)
        def _():
            step(i - 1, i % 2, i, sa_ref, sb_ref, masked=False)
            step(i, q_slot_after, 0, sb_ref, sa_ref, masked=True)
            finish(i)

        @pl.when(rest % 2 == 0)
        def _():
            step(i, q_slot_after, 0, sa_ref, sa_ref, masked=True)
            finish(i)

        return carry

    prep_q(0)
    k0 = k_ref[block(0), :]
    for r0 in range(0, 2 * tq, rc):
        scores_chunk(0, k0, sa_ref, r0)
    lax.fori_loop(0, nq, q_block, 0)


def _attn(proj, lq1, lk1, lq2, lk2, subln, batch, seq, lambda_init, *, tq=512, rc=256):
    n_tok = proj.shape[0]
    q_blk = 3 * (N_HEADS * DV) // LANES
    k_blk = q_blk + N_HEADS
    v_blk = k_blk + N_HEADS
    vec = lambda width: pl.BlockSpec((1, width), lambda b, h: (0, 0))
    head = lambda first_blk: pl.BlockSpec((seq, LANES), lambda b, h: (b, first_blk + h))
    return pl.pallas_call(
        functools.partial(_attn_kernel, tq=tq, rc=rc, lambda_init=lambda_init),
        grid=(batch, N_HEADS),
        in_specs=[vec(DK), vec(DK), vec(DK), vec(DK), vec(DV), head(q_blk), head(k_blk), head(v_blk)],
        out_specs=head(0),
        out_shape=jax.ShapeDtypeStruct((n_tok, N_HEADS * DV), BF16),
        scratch_shapes=[
            pltpu.VMEM((2, 2 * tq, LANES), BF16),
            pltpu.VMEM((2 * tq, tq), F32),
            pltpu.VMEM((2 * tq, tq), F32),
            pltpu.VMEM((2 * tq, LANES), F32),
            pltpu.VMEM((2 * tq, LANES), F32),
            pltpu.VMEM((2 * tq, DV), F32),
        ],
        compiler_params=pltpu.CompilerParams(
            dimension_semantics=("arbitrary", "arbitrary"),
            vmem_limit_bytes=48 * MIB),
        name="diff_attn",
    )(lq1, lk1, lq2, lk2, subln, proj, proj, proj)


def _out_proj_kernel(x_ref, yc_ref, ya_ref, wc_ref, wa_ref, o_ref):
    y = jnp.dot(yc_ref[...], wc_ref[...], preferred_element_type=F32)
    y += jnp.dot(ya_ref[...], wa_ref[...], preferred_element_type=F32)
    o_ref[...] = x_ref[...] + y


def _out_proj(x, y_conv, y_attn, w_out, *, tm=512):
    n_tok, d = x.shape
    cw = y_conv.shape[1]
    aw = y_attn.shape[1]
    return pl.pallas_call(
        _out_proj_kernel,
        grid=(n_tok // tm,),
        in_specs=[
            pl.BlockSpec((tm, d), lambda i: (i, 0)),
            pl.BlockSpec((tm, cw), lambda i: (i, 0)),
            pl.BlockSpec((tm, aw), lambda i: (i, 0)),
            pl.BlockSpec((cw, d), lambda i: (0, 0)),
            pl.BlockSpec((aw, d), lambda i: (1, 0)),
        ],
        out_specs=pl.BlockSpec((tm, d), lambda i: (i, 0)),
        out_shape=jax.ShapeDtypeStruct((n_tok, d), F32),
        compiler_params=pltpu.CompilerParams(
            dimension_semantics=("arbitrary",),
            vmem_limit_bytes=48 * MIB),
        name="out_proj",
    )(x, y_conv, y_attn, w_out, w_out)


def kernel(x, ffn1_norm, ffn1_w_gate, ffn1_w_up, ffn1_w_down, mix_norm, w_in, conv_w, lambda_q1, lambda_k1, lambda_q2, lambda_k2, subln_w, w_out, ffn2_norm, ffn2_w_gate, ffn2_w_up, ffn2_w_down, final_norm):
    batch, seq, d = x.shape
    depth = ffn1_norm.shape[0]
    assert w_in.shape[2] == 3 * conv_w.shape[2] + 3 * N_HEADS * DV
    assert conv_w.shape[2] == N_HEADS * DV
    h = x.reshape(batch * seq, d)
    row = lambda a: a.reshape(1, -1)
    for l in range(depth):
        last = l == depth - 1
        lambda_init = 0.8 - 0.6 * math.exp(-0.3 * l)
        h = _ffn(h, row(ffn1_norm[l]), ffn1_w_gate[l].astype(BF16), ffn1_w_up[l].astype(BF16),
                 ffn1_w_down[l].astype(BF16), row(final_norm), final_norm=False)
        proj = _in_proj(h, row(mix_norm[l]), w_in[l].astype(BF16))
        y_conv = _conv(proj, conv_w[l], seq)
        y_attn = _attn(proj, row(lambda_q1[l]), row(lambda_k1[l]), row(lambda_q2[l]), row(lambda_k2[l]),
                       row(subln_w[l]), batch, seq, lambda_init)
        h = _out_proj(h, y_conv, y_attn, w_out[l].astype(BF16))
        h = _ffn(h, row(ffn2_norm[l]), ffn2_w_gate[l].astype(BF16), ffn2_w_up[l].astype(BF16),
                 ffn2_w_down[l].astype(BF16), row(final_norm), final_norm=last)
    if depth == 0:
        raise ValueError("depth must be at least 1")
    return h.reshape(batch, seq, d)
```

```python
import functools
import math

import jax
import jax.numpy as jnp
from jax import lax
from jax.experimental import pallas as pl
from jax.experimental.pallas import tpu as pltpu

EPS = 1e-6
CONV_K = 3
N_HEADS = 8
DK = 64
DV = 128
LANES = 128
BF16_ROWS = 16
ROW_CHUNK = 128
CONV_COLS = 256
MIB = 1024 * 1024

F32 = jnp.float32
BF16 = jnp.bfloat16


def _rms(x, g):
    return x * lax.rsqrt(jnp.mean(x * x, axis=-1, keepdims=True) + EPS) * g


def _for_row_chunks(n_rows, fn):
    def chunk(c, carry):
        fn(pl.ds(pl.multiple_of(c * ROW_CHUNK, ROW_CHUNK), ROW_CHUNK))
        return carry
    lax.fori_loop(0, n_rows // ROW_CHUNK, chunk, 0)


def _ffn_kernel(x_ref, g_ref, wg_ref, wu_ref, wd_ref, fg_ref, o_ref, n_ref, *, final_norm):
    f = pl.program_id(1)
    for_row_chunks = functools.partial(_for_row_chunks, x_ref.shape[0])

    @pl.when(f == 0)
    def _():
        def norm_rows(rows):
            n_ref[rows, :] = _rms(x_ref[rows, :], g_ref[...]).astype(BF16)
            o_ref[rows, :] = jnp.zeros((ROW_CHUNK, o_ref.shape[1]), F32)
        for_row_chunks(norm_rows)

    n = n_ref[...]
    gate = jnp.dot(n, wg_ref[...], preferred_element_type=F32)
    up = jnp.dot(n, wu_ref[...], preferred_element_type=F32)
    h = (gate * jax.nn.sigmoid(gate) * up).astype(BF16)
    o_ref[...] += jnp.dot(h, wd_ref[...], preferred_element_type=F32)

    @pl.when(f == pl.num_programs(1) - 1)
    def _():
        def residual_rows(rows):
            y = x_ref[rows, :] + 0.5 * o_ref[rows, :]
            if final_norm:
                y = _rms(y, fg_ref[...])
            o_ref[rows, :] = y
        for_row_chunks(residual_rows)


def _ffn(x, gain, wg, wu, wd, final_gain, *, final_norm, tm=1024, tf=512):
    n_tok, d = x.shape
    ffn = wg.shape[1]
    return pl.pallas_call(
        functools.partial(_ffn_kernel, final_norm=final_norm),
        grid=(n_tok // tm, ffn // tf),
        in_specs=[
            pl.BlockSpec((tm, d), lambda i, f: (i, 0)),
            pl.BlockSpec((1, d), lambda i, f: (0, 0)),
            pl.BlockSpec((d, tf), lambda i, f: (0, f)),
            pl.BlockSpec((d, tf), lambda i, f: (0, f)),
            pl.BlockSpec((tf, d), lambda i, f: (f, 0)),
            pl.BlockSpec((1, d), lambda i, f: (0, 0)),
        ],
        out_specs=pl.BlockSpec((tm, d), lambda i, f: (i, 0)),
        out_shape=jax.ShapeDtypeStruct((n_tok, d), F32),
        scratch_shapes=[pltpu.VMEM((tm, d), BF16)],
        compiler_params=pltpu.CompilerParams(
            dimension_semantics=("arbitrary", "arbitrary"),
            vmem_limit_bytes=56 * MIB),
        name="ffn",
    )(x, gain, wg, wu, wd, final_gain)


def _in_proj_kernel(x_ref, g_ref, w_ref, o_ref, n_ref):
    @pl.when(pl.program_id(1) == 0)
    def _():
        def norm_rows(rows):
            n_ref[rows, :] = _rms(x_ref[rows, :], g_ref[...]).astype(BF16)
        _for_row_chunks(x_ref.shape[0], norm_rows)

    o_ref[...] = jnp.dot(n_ref[...], w_ref[...], preferred_element_type=F32).astype(o_ref.dtype)


def _in_proj(x, gain, w, *, tm=1024, tn=2048):
    n_tok, d = x.shape
    cols = w.shape[1]
    return pl.pallas_call(
        _in_proj_kernel,
        grid=(n_tok // tm, cols // tn),
        in_specs=[
            pl.BlockSpec((tm, d), lambda i, j: (i, 0)),
            pl.BlockSpec((1, d), lambda i, j: (0, 0)),
            pl.BlockSpec((d, tn), lambda i, j: (0, j)),
        ],
        out_specs=pl.BlockSpec((tm, tn), lambda i, j: (i, j)),
        out_shape=jax.ShapeDtypeStruct((n_tok, cols), BF16),
        scratch_shapes=[pltpu.VMEM((tm, d), BF16)],
        compiler_params=pltpu.CompilerParams(
            dimension_semantics=("arbitrary", "arbitrary"),
            vmem_limit_bytes=48 * MIB),
        name="in_proj",
    )(x, gain, w)


def _attn_kernel(*refs, n_cast, tq, rc, lambda_init):
    lq1_ref, lk1_ref, lq2_ref, lk2_ref, sw_ref, q_ref, k_ref, v_ref = refs[:8]
    o_ref = refs[8 + n_cast]
    q2_ref, sa_ref, sb_ref, m_ref, l_ref, acc_ref = refs[9 + 2 * n_cast:]
    for src_ref, dst_ref in zip(refs[8:8 + n_cast], refs[9 + n_cast:9 + 2 * n_cast]):
        dst_ref[...] = src_ref[...].astype(BF16)

    seq = q_ref.shape[0]
    nq = seq // tq
    scale = 1.0 / math.sqrt(DK)
    contract_last = (((1,), (1,)), ((), ()))

    def block(idx):
        return pl.ds(pl.multiple_of(idx * tq, tq), tq)

    def prep_q(i):
        q = q_ref[block(i), :]
        lane = lax.broadcasted_iota(jnp.int32, q.shape, 1)
        qs = (q.astype(F32) * scale).astype(BF16)
        zero = jnp.zeros_like(qs)
        q2_ref[i % 2, 0:tq, :] = jnp.where(lane < DK, qs, zero)
        q2_ref[i % 2, tq:2 * tq, :] = jnp.where(lane >= DK, qs, zero)

    def scores_chunk(q_slot, k, s_ref, r0):
        s_ref[r0:r0 + rc, :] = lax.dot_general(
            q2_ref[q_slot, r0:r0 + rc, :], k, contract_last, preferred_element_type=F32)

    def update_chunk(s_ref, v_ones, r0, masked):
        s = s_ref[r0:r0 + rc, :]
        if masked:
            q_pos = lax.broadcasted_iota(jnp.int32, (rc, tq), 0) + (r0 % tq)
            k_pos = lax.broadcasted_iota(jnp.int32, (rc, tq), 1)
            s = jnp.where(k_pos <= q_pos, s, -jnp.inf)
        cols = [s[:, c0:c0 + LANES] for c0 in range(0, tq, LANES)]
        lane_max = functools.reduce(jnp.maximum, cols)
        row_max = jnp.broadcast_to(jnp.max(lane_max, axis=-1, keepdims=True), (rc, LANES))
        rows = pl.ds(r0, rc)
        m_prev = m_ref[rows, :]
        m_new = jnp.maximum(m_prev, row_max)
        alpha = jnp.exp(m_prev - m_new)
        p = jnp.concatenate([jnp.exp(col - m_new) for col in cols], axis=1).astype(BF16)
        pv = jnp.dot(p, v_ones, preferred_element_type=F32)
        acc_ref[rows, :] = alpha * acc_ref[rows, :] + pv[:, :DV]
        l_ref[rows, :] = alpha * l_ref[rows, :] + pv[:, DV:]
        m_ref[rows, :] = m_new

    def step(j, q_slot_next, j_next, s_cur, s_next, masked):
        k_next = k_ref[block(j_next), :]
        v = v_ref[block(j), :]
        v_ones = jnp.concatenate([v, jnp.ones_like(v)], axis=1)
        for r0 in range(0, 2 * tq, rc):
            if s_next is s_cur:
                update_chunk(s_cur, v_ones, r0, masked)
                scores_chunk(q_slot_next, k_next, s_next, r0)
            else:
                scores_chunk(q_slot_next, k_next, s_next, r0)
                update_chunk(s_cur, v_ones, r0, masked)

    def finish(i):
        lam = (jnp.exp(jnp.sum(lq1_ref[...] * lk1_ref[...], axis=-1, keepdims=True))
               - jnp.exp(jnp.sum(lq2_ref[...] * lk2_ref[...], axis=-1, keepdims=True))
               + lambda_init)
        o_all = acc_ref[...] / l_ref[...]
        o = o_all[0:tq] - lam * o_all[tq:2 * tq]
        o_ref[block(i), :] = (_rms(o, sw_ref[...]) * (1.0 - lambda_init)).astype(o_ref.dtype)

    def straight_steps(i, j, count):
        for u in range(0, count, 2):
            step(j + u, i % 2, j + u + 1, sa_ref, sb_ref, masked=False)
            step(j + u + 1, i % 2, j + u + 2, sb_ref, sa_ref, masked=False)

    def q_block(i, carry):
        m_ref[...] = jnp.full_like(m_ref, -jnp.inf)
        l_ref[...] = jnp.zeros_like(l_ref)
        acc_ref[...] = jnp.zeros_like(acc_ref)

        @pl.when(i + 1 < nq)
        def _():
            prep_q(i + 1)

        def quad(g, c):
            straight_steps(i, 4 * g, 4)
            return c

        n_quads = i // 4
        lax.fori_loop(0, n_quads, quad, 0)
        rest = i - 4 * n_quads
        q_slot_after = jnp.minimum(i + 1, nq - 1) % 2

        @pl.when(rest >= 2)
        def _():
            straight_steps(i, 4 * n_quads, 2)

        @pl.when(rest % 2 == 1)
        def _():
            step(i - 1, i % 2, i, sa_ref, sb_ref, masked=False)
            step(i, q_slot_after, 0, sb_ref, sa_ref, masked=True)
            finish(i)

        @pl.when(rest % 2 == 0)
        def _():
            step(i, q_slot_after, 0, sa_ref, sa_ref, masked=True)
            finish(i)

        return carry

    prep_q(0)
    k0 = k_ref[block(0), :]
    for r0 in range(0, 2 * tq, rc):
        scores_chunk(0, k0, sa_ref, r0)
    lax.fori_loop(0, nq, q_block, 0)


def _attn(proj, lq1, lk1, lq2, lk2, subln, batch, seq, lambda_init, cast_weights, *, tq=512, rc=256):
    n_tok = proj.shape[0]
    q_blk = 3 * (N_HEADS * DV) // LANES
    k_blk = q_blk + N_HEADS
    v_blk = k_blk + N_HEADS
    vec = lambda width: pl.BlockSpec((1, width), lambda b, h: (0, 0))
    head = lambda first_blk: pl.BlockSpec((seq, LANES), lambda b, h: (b, first_blk + h))
    n_grid = batch * N_HEADS
    for w in cast_weights:
        assert w.shape[0] % (n_grid * BF16_ROWS) == 0, w.shape
    slab = lambda w: pl.BlockSpec((w.shape[0] // n_grid, w.shape[1]), lambda b, h: (b * N_HEADS + h, 0))
    outs = pl.pallas_call(
        functools.partial(_attn_kernel, n_cast=len(cast_weights), tq=tq, rc=rc, lambda_init=lambda_init),
        grid=(batch, N_HEADS),
        in_specs=[vec(DK), vec(DK), vec(DK), vec(DK), vec(DV), head(q_blk), head(k_blk), head(v_blk)]
        + [slab(w) for w in cast_weights],
        out_specs=[head(0)] + [slab(w) for w in cast_weights],
        out_shape=[jax.ShapeDtypeStruct((n_tok, N_HEADS * DV), BF16)]
        + [jax.ShapeDtypeStruct(w.shape, BF16) for w in cast_weights],
        scratch_shapes=[
            pltpu.VMEM((2, 2 * tq, LANES), BF16),
            pltpu.VMEM((2 * tq, tq), F32),
            pltpu.VMEM((2 * tq, tq), F32),
            pltpu.VMEM((2 * tq, LANES), F32),
            pltpu.VMEM((2 * tq, LANES), F32),
            pltpu.VMEM((2 * tq, DV), F32),
        ],
        compiler_params=pltpu.CompilerParams(
            dimension_semantics=("arbitrary", "arbitrary"),
            vmem_limit_bytes=48 * MIB),
        name="diff_attn",
    )(lq1, lk1, lq2, lk2, subln, proj, proj, proj, *cast_weights)
    return outs[0], outs[1:]


def _conv_taps(cx_m2, cx_m1, cx, w):
    return w[0:1, :] * cx_m2 + w[1:2, :] * cx_m1 + w[2:3, :] * cx


def _out_proj_kernel(x_ref, b_ref, c_ref, xv_ref, ch_ref, xh_ref, cw_ref, ya_ref, wc_ref, wa_ref, o_ref,
                     yc_ref, *, tiles_per_seq):
    h = BF16_ROWS
    at_seq_start = pl.program_id(0) % tiles_per_seq == 0
    row = lax.broadcasted_iota(jnp.int32, (h, 1), 0)
    for c0 in range(0, cw_ref.shape[1], CONV_COLS):
        cols = slice(c0, c0 + CONV_COLS)
        w = cw_ref[:, cols]
        cx = c_ref[:, cols].astype(F32) * xv_ref[:, cols].astype(F32)
        y = _conv_taps(pltpu.roll(cx, 2, axis=0), pltpu.roll(cx, 1, axis=0), cx, w)
        yc_ref[:, cols] = (b_ref[:, cols].astype(F32) * y).astype(BF16)
        halo = ch_ref[:, cols].astype(F32) * xh_ref[:, cols].astype(F32)
        halo = jnp.where(at_seq_start, 0.0, halo)
        top = cx[:h]
        m1 = jnp.where(row < 1, pltpu.roll(halo, 1, axis=0), pltpu.roll(top, 1, axis=0))
        m2 = jnp.where(row < 2, pltpu.roll(halo, 2, axis=0), pltpu.roll(top, 2, axis=0))
        yc_ref[0:h, cols] = (b_ref[0:h, cols].astype(F32) * _conv_taps(m2, m1, top, w)).astype(BF16)
    y = jnp.dot(ya_ref[...], wa_ref[...], preferred_element_type=F32)
    y += jnp.dot(yc_ref[...], wc_ref[...], preferred_element_type=F32)
    o_ref[...] = x_ref[...] + y


def _out_proj(x, proj, y_attn, conv_w, w_out, seq, *, tm=512):
    n_tok, d = x.shape
    cw = conv_w.shape[1]
    aw = y_attn.shape[1]
    halo_blocks = tm // BF16_ROWS

    def halo_map(sec):
        return lambda i: (jnp.maximum(i * halo_blocks - 1, 0), sec)

    resident = dict(pipeline_mode=pl.Buffered(1))
    return pl.pallas_call(
        functools.partial(_out_proj_kernel, tiles_per_seq=seq // tm),
        grid=(n_tok // tm,),
        in_specs=[
            pl.BlockSpec((tm, d), lambda i: (i, 0)),
            pl.BlockSpec((tm, cw), lambda i: (i, 0)),
            pl.BlockSpec((tm, cw), lambda i: (i, 1)),
            pl.BlockSpec((tm, cw), lambda i: (i, 2)),
            pl.BlockSpec((BF16_ROWS, cw), halo_map(1)),
            pl.BlockSpec((BF16_ROWS, cw), halo_map(2)),
            pl.BlockSpec((CONV_K, cw), lambda i: (0, 0)),
            pl.BlockSpec((tm, aw), lambda i: (i, 0)),
            pl.BlockSpec((cw, d), lambda i: (0, 0), **resident),
            pl.BlockSpec((aw, d), lambda i: (1, 0), **resident),
        ],
        out_specs=pl.BlockSpec((tm, d), lambda i: (i, 0)),
        out_shape=jax.ShapeDtypeStruct((n_tok, d), F32),
        scratch_shapes=[pltpu.VMEM((tm, cw), BF16)],
        compiler_params=pltpu.CompilerParams(
            dimension_semantics=("arbitrary",),
            vmem_limit_bytes=48 * MIB),
        name="out_proj",
    )(x, proj, proj, proj, proj, proj, conv_w, y_attn, w_out, w_out)


def kernel(x, ffn1_norm, ffn1_w_gate, ffn1_w_up, ffn1_w_down, mix_norm, w_in, conv_w, lambda_q1, lambda_k1, lambda_q2, lambda_k2, subln_w, w_out, ffn2_norm, ffn2_w_gate, ffn2_w_up, ffn2_w_down, final_norm):
    batch, seq, d = x.shape
    depth = ffn1_norm.shape[0]
    assert w_in.shape[2] == 3 * conv_w.shape[2] + 3 * N_HEADS * DV
    assert conv_w.shape[2] == N_HEADS * DV
    h = x.reshape(batch * seq, d)
    row = lambda a: a.reshape(1, -1)
    for l in range(depth):
        last = l == depth - 1
        lambda_init = 0.8 - 0.6 * math.exp(-0.3 * l)
        h = _ffn(h, row(ffn1_norm[l]), ffn1_w_gate[l].astype(BF16), ffn1_w_up[l].astype(BF16),
                 ffn1_w_down[l].astype(BF16), row(final_norm), final_norm=False)
        proj = _in_proj(h, row(mix_norm[l]), w_in[l].astype(BF16))
        y_attn, (w_out_b, wg2, wu2, wd2) = _attn(
            proj, row(lambda_q1[l]), row(lambda_k1[l]), row(lambda_q2[l]), row(lambda_k2[l]), row(subln_w[l]),
            batch, seq, lambda_init, [w_out[l], ffn2_w_gate[l], ffn2_w_up[l], ffn2_w_down[l]])
        h = _out_proj(h, proj, y_attn, conv_w[l], w_out_b, seq)
        h = _ffn(h, row(ffn2_norm[l]), wg2, wu2, wd2, row(final_norm), final_norm=last)
    if depth == 0:
        raise ValueError("depth must be at least 1")
    return h.reshape(batch, seq, d)
```

```python
import functools
import math

import jax
import jax.numpy as jnp
from jax import lax
from jax.experimental import pallas as pl
from jax.experimental.pallas import tpu as pltpu

EPS = 1e-6
CONV_K = 3
N_HEADS = 8
DK = 64
DV = 128
LANES = 128
BF16_ROWS = 16
FFN_ROWS = 256
CONV_COLS = 256
MIB = 1024 * 1024

F32 = jnp.float32
BF16 = jnp.bfloat16


def _rms(x, g):
    return x * lax.rsqrt(jnp.mean(x * x, axis=-1, keepdims=True) + EPS) * g


def _ffn_kernel(x_ref, g_ref, wg_ref, wu_ref, wd_ref, fg_ref, o_ref, n_ref, *, final_norm):
    f = pl.program_id(1)
    last = pl.num_programs(1) - 1
    chunks = [slice(r0, r0 + FFN_ROWS) for r0 in range(0, x_ref.shape[0], FFN_ROWS)]

    def swiglu_down(n):
        gate = jnp.dot(n, wg_ref[...], preferred_element_type=F32)
        up = jnp.dot(n, wu_ref[...], preferred_element_type=F32)
        h = (gate * jax.nn.sigmoid(gate) * up).astype(BF16)
        return jnp.dot(h, wd_ref[...], preferred_element_type=F32)

    @pl.when(f == 0)
    def _():
        for rows in chunks:
            n = _rms(x_ref[rows, :], g_ref[...]).astype(BF16)
            n_ref[rows, :] = n
            o_ref[rows, :] = swiglu_down(n)

    @pl.when(jnp.logical_and(f > 0, f < last))
    def _():
        o_ref[...] += swiglu_down(n_ref[...])

    @pl.when(f == last)
    def _():
        for rows in chunks:
            y = x_ref[rows, :] + 0.5 * (o_ref[rows, :] + swiglu_down(n_ref[rows, :]))
            if final_norm:
                y = _rms(y, fg_ref[...])
            o_ref[rows, :] = y


def _ffn(x, gain, wg, wu, wd, final_gain, *, final_norm, tm=1024, tf=512):
    n_tok, d = x.shape
    ffn = wg.shape[1]
    assert ffn // tf >= 2
    return pl.pallas_call(
        functools.partial(_ffn_kernel, final_norm=final_norm),
        grid=(n_tok // tm, ffn // tf),
        in_specs=[
            pl.BlockSpec((tm, d), lambda i, f: (i, 0)),
            pl.BlockSpec((1, d), lambda i, f: (0, 0)),
            pl.BlockSpec((d, tf), lambda i, f: (0, f)),
            pl.BlockSpec((d, tf), lambda i, f: (0, f)),
            pl.BlockSpec((tf, d), lambda i, f: (f, 0)),
            pl.BlockSpec((1, d), lambda i, f: (0, 0)),
        ],
        out_specs=pl.BlockSpec((tm, d), lambda i, f: (i, 0)),
        out_shape=jax.ShapeDtypeStruct((n_tok, d), F32),
        scratch_shapes=[pltpu.VMEM((tm, d), BF16)],
        compiler_params=pltpu.CompilerParams(
            dimension_semantics=("arbitrary", "arbitrary"),
            vmem_limit_bytes=56 * MIB),
        name="ffn",
    )(x, gain, wg, wu, wd, final_gain)


def _in_proj_kernel(x_ref, g_ref, w_ref, o_ref, n_ref):
    j = pl.program_id(1)

    @pl.when(j == 0)
    def _():
        for r0 in range(0, x_ref.shape[0], FFN_ROWS):
            rows = slice(r0, r0 + FFN_ROWS)
            n = _rms(x_ref[rows, :], g_ref[...]).astype(BF16)
            n_ref[rows, :] = n
            o_ref[rows, :] = jnp.dot(n, w_ref[...], preferred_element_type=F32).astype(o_ref.dtype)

    @pl.when(j > 0)
    def _():
        o_ref[...] = jnp.dot(n_ref[...], w_ref[...], preferred_element_type=F32).astype(o_ref.dtype)


def _in_proj(x, gain, w, *, tm=1024, tn=2048):
    n_tok, d = x.shape
    cols = w.shape[1]
    return pl.pallas_call(
        _in_proj_kernel,
        grid=(n_tok // tm, cols // tn),
        in_specs=[
            pl.BlockSpec((tm, d), lambda i, j: (i, 0)),
            pl.BlockSpec((1, d), lambda i, j: (0, 0)),
            pl.BlockSpec((d, tn), lambda i, j: (0, j)),
        ],
        out_specs=pl.BlockSpec((tm, tn), lambda i, j: (i, j)),
        out_shape=jax.ShapeDtypeStruct((n_tok, cols), BF16),
        scratch_shapes=[pltpu.VMEM((tm, d), BF16)],
        compiler_params=pltpu.CompilerParams(
            dimension_semantics=("arbitrary", "arbitrary"),
            vmem_limit_bytes=48 * MIB),
        name="in_proj",
    )(x, gain, w)


def _attn_kernel(*refs, n_cast, tq, rc, lambda_init):
    lq1_ref, lk1_ref, lq2_ref, lk2_ref, sw_ref, q_ref, k_ref, v_ref = refs[:8]
    o_ref = refs[8 + n_cast]
    q2_ref, sa_ref, sb_ref, m_ref, l_ref, acc_ref = refs[9 + 2 * n_cast:]
    for src_ref, dst_ref in zip(refs[8:8 + n_cast], refs[9 + n_cast:9 + 2 * n_cast]):
        dst_ref[...] = src_ref[...].astype(BF16)

    seq = q_ref.shape[0]
    nq = seq // tq
    scale = 1.0 / math.sqrt(DK)
    contract_last = (((1,), (1,)), ((), ()))

    def block(idx):
        return pl.ds(pl.multiple_of(idx * tq, tq), tq)

    def prep_q(i):
        q = q_ref[block(i), :]
        lane = lax.broadcasted_iota(jnp.int32, q.shape, 1)
        qs = (q.astype(F32) * scale).astype(BF16)
        zero = jnp.zeros_like(qs)
        q2_ref[i % 2, 0:tq, :] = jnp.where(lane < DK, qs, zero)
        q2_ref[i % 2, tq:2 * tq, :] = jnp.where(lane >= DK, qs, zero)

    def scores_chunk(q_slot, k, s_ref, r0):
        s_ref[r0:r0 + rc, :] = lax.dot_general(
            q2_ref[q_slot, r0:r0 + rc, :], k, contract_last, preferred_element_type=F32)

    def update_chunk(s_ref, v_ones, r0, masked):
        s = s_ref[r0:r0 + rc, :]
        if masked:
            q_pos = lax.broadcasted_iota(jnp.int32, (rc, tq), 0) + (r0 % tq)
            k_pos = lax.broadcasted_iota(jnp.int32, (rc, tq), 1)
            s = jnp.where(k_pos <= q_pos, s, -jnp.inf)
        cols = [s[:, c0:c0 + LANES] for c0 in range(0, tq, LANES)]
        lane_max = functools.reduce(jnp.maximum, cols)
        row_max = jnp.broadcast_to(jnp.max(lane_max, axis=-1, keepdims=True), (rc, LANES))
        rows = pl.ds(r0, rc)
        m_prev = m_ref[rows, :]
        m_new = jnp.maximum(m_prev, row_max)
        alpha = jnp.exp(m_prev - m_new)
        p = jnp.concatenate([jnp.exp(col - m_new) for col in cols], axis=1).astype(BF16)
        pv = jnp.dot(p, v_ones, preferred_element_type=F32)
        acc_ref[rows, :] = alpha * acc_ref[rows, :] + pv[:, :DV]
        l_ref[rows, :] = alpha * l_ref[rows, :] + pv[:, DV:]
        m_ref[rows, :] = m_new

    def step(j, q_slot_next, j_next, s_cur, s_next, masked):
        k_next = k_ref[block(j_next), :]
        v = v_ref[block(j), :]
        v_ones = jnp.concatenate([v, jnp.ones_like(v)], axis=1)
        for r0 in range(0, 2 * tq, rc):
            if s_next is s_cur:
                update_chunk(s_cur, v_ones, r0, masked)
                scores_chunk(q_slot_next, k_next, s_next, r0)
            else:
                scores_chunk(q_slot_next, k_next, s_next, r0)
                update_chunk(s_cur, v_ones, r0, masked)

    def finish(i):
        lam = (jnp.exp(jnp.sum(lq1_ref[...] * lk1_ref[...], axis=-1, keepdims=True))
               - jnp.exp(jnp.sum(lq2_ref[...] * lk2_ref[...], axis=-1, keepdims=True))
               + lambda_init)
        o_all = acc_ref[...] / l_ref[...]
        o = o_all[0:tq] - lam * o_all[tq:2 * tq]
        o_ref[block(i), :] = (_rms(o, sw_ref[...]) * (1.0 - lambda_init)).astype(o_ref.dtype)

    def straight_steps(i, j, count):
        for u in range(0, count, 2):
            step(j + u, i % 2, j + u + 1, sa_ref, sb_ref, masked=False)
            step(j + u + 1, i % 2, j + u + 2, sb_ref, sa_ref, masked=False)

    def q_block(i, carry):
        m_ref[...] = jnp.full_like(m_ref, -jnp.inf)
        l_ref[...] = jnp.zeros_like(l_ref)
        acc_ref[...] = jnp.zeros_like(acc_ref)

        @pl.when(i + 1 < nq)
        def _():
            prep_q(i + 1)

        def quad(g, c):
            straight_steps(i, 4 * g, 4)
            return c

        n_quads = i // 4
        lax.fori_loop(0, n_quads, quad, 0)
        rest = i - 4 * n_quads
        q_slot_after = jnp.minimum(i + 1, nq - 1) % 2

        @pl.when(rest >= 2)
        def _():
            straight_steps(i, 4 * n_quads, 2)

        @pl.when(rest % 2 == 1)
        def _():
            step(i - 1, i % 2, i, sa_ref, sb_ref, masked=False)
            step(i, q_slot_after, 0, sb_ref, sa_ref, masked=True)
            finish(i)

        @pl.when(rest % 2 == 0)
        def _():
            step(i, q_slot_after, 0, sa_ref, sa_ref, masked=True)
            finish(i)

        return carry

    prep_q(0)
    k0 = k_ref[block(0), :]
    for r0 in range(0, 2 * tq, rc):
        scores_chunk(0, k0, sa_ref, r0)
    lax.fori_loop(0, nq, q_block, 0)


def _attn(proj, lq1, lk1, lq2, lk2, subln, batch, seq, lambda_init, cast_weights, *, tq=512, rc=256):
    n_tok = proj.shape[0]
    q_blk = 3 * (N_HEADS * DV) // LANES
    k_blk = q_blk + N_HEADS
    v_blk = k_blk + N_HEADS
    vec = lambda width: pl.BlockSpec((1, width), lambda b, h: (0, 0))
    head = lambda first_blk: pl.BlockSpec((seq, LANES), lambda b, h: (b, first_blk + h))
    n_grid = batch * N_HEADS
    for w in cast_weights:
        assert w.shape[0] % (n_grid * BF16_ROWS) == 0, w.shape
    slab = lambda w: pl.BlockSpec((w.shape[0] // n_grid, w.shape[1]), lambda b, h: (b * N_HEADS + h, 0))
    outs = pl.pallas_call(
        functools.partial(_attn_kernel, n_cast=len(cast_weights), tq=tq, rc=rc, lambda_init=lambda_init),
        grid=(batch, N_HEADS),
        in_specs=[vec(DK), vec(DK), vec(DK), vec(DK), vec(DV), head(q_blk), head(k_blk), head(v_blk)]
        + [slab(w) for w in cast_weights],
        out_specs=[head(0)] + [slab(w) for w in cast_weights],
        out_shape=[jax.ShapeDtypeStruct((n_tok, N_HEADS * DV), BF16)]
        + [jax.ShapeDtypeStruct(w.shape, BF16) for w in cast_weights],
        scratch_shapes=[
            pltpu.VMEM((2, 2 * tq, LANES), BF16),
            pltpu.VMEM((2 * tq, tq), F32),
            pltpu.VMEM((2 * tq, tq), F32),
            pltpu.VMEM((2 * tq, LANES), F32),
            pltpu.VMEM((2 * tq, LANES), F32),
            pltpu.VMEM((2 * tq, DV), F32),
        ],
        compiler_params=pltpu.CompilerParams(
            dimension_semantics=("arbitrary", "arbitrary"),
            vmem_limit_bytes=48 * MIB),
        name="diff_attn",
    )(lq1, lk1, lq2, lk2, subln, proj, proj, proj, *cast_weights)
    return outs[0], outs[1:]


def _conv_taps(cx_m2, cx_m1, cx, w):
    return w[0:1, :] * cx_m2 + w[1:2, :] * cx_m1 + w[2:3, :] * cx


def _out_proj_kernel(x_ref, b_ref, c_ref, xv_ref, ch_ref, xh_ref, cw_ref, ya_ref, wc_ref, wa_ref, o_ref,
                     yc_ref, *, tiles_per_seq):
    h = BF16_ROWS
    at_seq_start = pl.program_id(0) % tiles_per_seq == 0
    row = lax.broadcasted_iota(jnp.int32, (h, 1), 0)
    for c0 in range(0, cw_ref.shape[1], CONV_COLS):
        cols = slice(c0, c0 + CONV_COLS)
        w = cw_ref[:, cols]
        cx = c_ref[:, cols].astype(F32) * xv_ref[:, cols].astype(F32)
        y = _conv_taps(pltpu.roll(cx, 2, axis=0), pltpu.roll(cx, 1, axis=0), cx, w)
        yc_ref[:, cols] = (b_ref[:, cols].astype(F32) * y).astype(BF16)
        halo = ch_ref[:, cols].astype(F32) * xh_ref[:, cols].astype(F32)
        halo = jnp.where(at_seq_start, 0.0, halo)
        top = cx[:h]
        m1 = jnp.where(row < 1, pltpu.roll(halo, 1, axis=0), pltpu.roll(top, 1, axis=0))
        m2 = jnp.where(row < 2, pltpu.roll(halo, 2, axis=0), pltpu.roll(top, 2, axis=0))
        yc_ref[0:h, cols] = (b_ref[0:h, cols].astype(F32) * _conv_taps(m2, m1, top, w)).astype(BF16)
    y = jnp.dot(ya_ref[...], wa_ref[...], preferred_element_type=F32)
    y += jnp.dot(yc_ref[...], wc_ref[...], preferred_element_type=F32)
    o_ref[...] = x_ref[...] + y


def _out_proj(x, proj, y_attn, conv_w, w_out, seq, *, tm=512):
    n_tok, d = x.shape
    cw = conv_w.shape[1]
    aw = y_attn.shape[1]
    halo_blocks = tm // BF16_ROWS

    def halo_map(sec):
        return lambda i: (jnp.maximum(i * halo_blocks - 1, 0), sec)

    resident = dict(pipeline_mode=pl.Buffered(1))
    return pl.pallas_call(
        functools.partial(_out_proj_kernel, tiles_per_seq=seq // tm),
        grid=(n_tok // tm,),
        in_specs=[
            pl.BlockSpec((tm, d), lambda i: (i, 0)),
            pl.BlockSpec((tm, cw), lambda i: (i, 0)),
            pl.BlockSpec((tm, cw), lambda i: (i, 1)),
            pl.BlockSpec((tm, cw), lambda i: (i, 2)),
            pl.BlockSpec((BF16_ROWS, cw), halo_map(1)),
            pl.BlockSpec((BF16_ROWS, cw), halo_map(2)),
            pl.BlockSpec((CONV_K, cw), lambda i: (0, 0)),
            pl.BlockSpec((tm, aw), lambda i: (i, 0)),
            pl.BlockSpec((cw, d), lambda i: (0, 0), **resident),
            pl.BlockSpec((aw, d), lambda i: (1, 0), **resident),
        ],
        out_specs=pl.BlockSpec((tm, d), lambda i: (i, 0)),
        out_shape=jax.ShapeDtypeStruct((n_tok, d), F32),
        scratch_shapes=[pltpu.VMEM((tm, cw), BF16)],
        compiler_params=pltpu.CompilerParams(
            dimension_semantics=("arbitrary",),
            vmem_limit_bytes=48 * MIB),
        name="out_proj",
    )(x, proj, proj, proj, proj, proj, conv_w, y_attn, w_out, w_out)


def kernel(x, ffn1_norm, ffn1_w_gate, ffn1_w_up, ffn1_w_down, mix_norm, w_in, conv_w, lambda_q1, lambda_k1, lambda_q2, lambda_k2, subln_w, w_out, ffn2_norm, ffn2_w_gate, ffn2_w_up, ffn2_w_down, final_norm):
    batch, seq, d = x.shape
    depth = ffn1_norm.shape[0]
    assert w_in.shape[2] == 3 * conv_w.shape[2] + 3 * N_HEADS * DV
    assert conv_w.shape[2] == N_HEADS * DV
    h = x.reshape(batch * seq, d)
    row = lambda a: a.reshape(1, -1)
    for l in range(depth):
        last = l == depth - 1
        lambda_init = 0.8 - 0.6 * math.exp(-0.3 * l)
        h = _ffn(h, row(ffn1_norm[l]), ffn1_w_gate[l].astype(BF16), ffn1_w_up[l].astype(BF16),
                 ffn1_w_down[l].astype(BF16), row(final_norm), final_norm=False)
        proj = _in_proj(h, row(mix_norm[l]), w_in[l].astype(BF16))
        y_attn, (w_out_b, wg2, wu2, wd2) = _attn(
            proj, row(lambda_q1[l]), row(lambda_k1[l]), row(lambda_q2[l]), row(lambda_k2[l]), row(subln_w[l]),
            batch, seq, lambda_init, [w_out[l], ffn2_w_gate[l], ffn2_w_up[l], ffn2_w_down[l]])
        h = _out_proj(h, proj, y_attn, conv_w[l], w_out_b, seq)
        h = _ffn(h, row(ffn2_norm[l]), wg2, wu2, wd2, row(final_norm), final_norm=last)
    if depth == 0:
        raise ValueError("depth must be at least 1")
    return h.reshape(batch, seq, d)
```

```python
import functools
import math

import jax
import jax.numpy as jnp
from jax import lax
from jax.experimental import pallas as pl
from jax.experimental.pallas import tpu as pltpu

EPS = 1e-6
CONV_K = 3
N_HEADS = 8
DK = 64
DV = 128
LANES = 128
BF16_ROWS = 16
FFN_ROWS = 256
FFN_COLS = 512
CONV_COLS = 256
MIB = 1024 * 1024

F32 = jnp.float32
BF16 = jnp.bfloat16


def _rms(x, g):
    return x * lax.rsqrt(jnp.mean(x * x, axis=-1, keepdims=True) + EPS) * g


def _ffn_kernel(x_ref, g_ref, wg_ref, wu_ref, wd_ref, fg_ref, o_ref, n_ref, *, final_norm):
    f = pl.program_id(1)
    last = pl.num_programs(1) - 1
    chunks = [slice(r0, r0 + FFN_ROWS) for r0 in range(0, x_ref.shape[0], FFN_ROWS)]

    def swiglu_down(n):
        gate = jnp.dot(n, wg_ref[...], preferred_element_type=F32)
        up = jnp.dot(n, wu_ref[...], preferred_element_type=F32)
        h = (gate * jax.nn.sigmoid(gate) * up).astype(BF16)
        return jnp.dot(h, wd_ref[...], preferred_element_type=F32)

    @pl.when(f == 0)
    def _():
        for rows in chunks:
            n = _rms(x_ref[rows, :], g_ref[...]).astype(BF16)
            n_ref[rows, :] = n
            o_ref[rows, :] = swiglu_down(n)

    @pl.when(jnp.logical_and(f > 0, f < last))
    def _():
        o_ref[...] += swiglu_down(n_ref[...])

    @pl.when(f == last)
    def _():
        for rows in chunks:
            y = x_ref[rows, :] + 0.5 * (o_ref[rows, :] + swiglu_down(n_ref[rows, :]))
            if final_norm:
                y = _rms(y, fg_ref[...])
            o_ref[rows, :] = y


def _column_tiles(w, tf):
    d, ffn = w.shape
    return w.reshape(d, ffn // tf, tf).transpose(1, 0, 2)


def _ffn(x, gain, wg, wu, wd, final_gain, *, final_norm, tm=1024):
    n_tok, d = x.shape
    n_tiles, _, tf = wg.shape
    assert n_tiles >= 2
    return pl.pallas_call(
        functools.partial(_ffn_kernel, final_norm=final_norm),
        grid=(n_tok // tm, n_tiles),
        in_specs=[
            pl.BlockSpec((tm, d), lambda i, f: (i, 0)),
            pl.BlockSpec((1, d), lambda i, f: (0, 0)),
            pl.BlockSpec((None, d, tf), lambda i, f: (f, 0, 0)),
            pl.BlockSpec((None, d, tf), lambda i, f: (f, 0, 0)),
            pl.BlockSpec((tf, d), lambda i, f: (f, 0)),
            pl.BlockSpec((1, d), lambda i, f: (0, 0)),
        ],
        out_specs=pl.BlockSpec((tm, d), lambda i, f: (i, 0)),
        out_shape=jax.ShapeDtypeStruct((n_tok, d), F32),
        scratch_shapes=[pltpu.VMEM((tm, d), BF16)],
        compiler_params=pltpu.CompilerParams(
            dimension_semantics=("arbitrary", "arbitrary"),
            vmem_limit_bytes=56 * MIB),
        name="ffn",
    )(x, gain, wg, wu, wd, final_gain)


def _in_proj_kernel(x_ref, g_ref, w_ref, o_ref, n_ref):
    j = pl.program_id(1)

    @pl.when(j == 0)
    def _():
        for r0 in range(0, x_ref.shape[0], FFN_ROWS):
            rows = slice(r0, r0 + FFN_ROWS)
            n = _rms(x_ref[rows, :], g_ref[...]).astype(BF16)
            n_ref[rows, :] = n
            o_ref[rows, :] = jnp.dot(n, w_ref[...], preferred_element_type=F32).astype(o_ref.dtype)

    @pl.when(j > 0)
    def _():
        o_ref[...] = jnp.dot(n_ref[...], w_ref[...], preferred_element_type=F32).astype(o_ref.dtype)


def _in_proj(x, gain, w, *, tm=1024, tn=2048):
    n_tok, d = x.shape
    cols = w.shape[1]
    return pl.pallas_call(
        _in_proj_kernel,
        grid=(n_tok // tm, cols // tn),
        in_specs=[
            pl.BlockSpec((tm, d), lambda i, j: (i, 0)),
            pl.BlockSpec((1, d), lambda i, j: (0, 0)),
            pl.BlockSpec((d, tn), lambda i, j: (0, j)),
        ],
        out_specs=pl.BlockSpec((tm, tn), lambda i, j: (i, j)),
        out_shape=jax.ShapeDtypeStruct((n_tok, cols), BF16),
        scratch_shapes=[pltpu.VMEM((tm, d), BF16)],
        compiler_params=pltpu.CompilerParams(
            dimension_semantics=("arbitrary", "arbitrary"),
            vmem_limit_bytes=48 * MIB),
        name="in_proj",
    )(x, gain, w)


def _attn_kernel(*refs, n_cast, tq, rc, lambda_init):
    lq1_ref, lk1_ref, lq2_ref, lk2_ref, sw_ref, q_ref, k_ref, v_ref = refs[:8]
    o_ref = refs[8 + n_cast]
    q2_ref, sa_ref, sb_ref, m_ref, l_ref, acc_ref = refs[9 + 2 * n_cast:]
    for src_ref, dst_ref in zip(refs[8:8 + n_cast], refs[9 + n_cast:9 + 2 * n_cast]):
        if len(dst_ref.shape) == 3:
            tf = dst_ref.shape[2]
            for t in range(dst_ref.shape[0]):
                dst_ref[t] = src_ref[:, t * tf:(t + 1) * tf].astype(BF16)
        else:
            dst_ref[...] = src_ref[...].astype(BF16)

    seq = q_ref.shape[0]
    nq = seq // tq
    scale = 1.0 / math.sqrt(DK)
    contract_last = (((1,), (1,)), ((), ()))

    def block(idx):
        return pl.ds(pl.multiple_of(idx * tq, tq), tq)

    def prep_q(i):
        q = q_ref[block(i), :]
        lane = lax.broadcasted_iota(jnp.int32, q.shape, 1)
        qs = (q.astype(F32) * scale).astype(BF16)
        zero = jnp.zeros_like(qs)
        q2_ref[i % 2, 0:tq, :] = jnp.where(lane < DK, qs, zero)
        q2_ref[i % 2, tq:2 * tq, :] = jnp.where(lane >= DK, qs, zero)

    def scores_chunk(q_slot, k, s_ref, r0):
        s_ref[r0:r0 + rc, :] = lax.dot_general(
            q2_ref[q_slot, r0:r0 + rc, :], k, contract_last, preferred_element_type=F32)

    def update_chunk(s_ref, v_ones, r0, masked):
        s = s_ref[r0:r0 + rc, :]
        if masked:
            q_pos = lax.broadcasted_iota(jnp.int32, (rc, tq), 0) + (r0 % tq)
            k_pos = lax.broadcasted_iota(jnp.int32, (rc, tq), 1)
            s = jnp.where(k_pos <= q_pos, s, -jnp.inf)
        cols = [s[:, c0:c0 + LANES] for c0 in range(0, tq, LANES)]
        lane_max = functools.reduce(jnp.maximum, cols)
        row_max = jnp.broadcast_to(jnp.max(lane_max, axis=-1, keepdims=True), (rc, LANES))
        rows = pl.ds(r0, rc)
        m_prev = m_ref[rows, :]
        m_new = jnp.maximum(m_prev, row_max)
        alpha = jnp.exp(m_prev - m_new)
        p = jnp.concatenate([jnp.exp(col - m_new) for col in cols], axis=1).astype(BF16)
        pv = jnp.dot(p, v_ones, preferred_element_type=F32)
        acc_ref[rows, :] = alpha * acc_ref[rows, :] + pv[:, :DV]
        l_ref[rows, :] = alpha * l_ref[rows, :] + pv[:, DV:]
        m_ref[rows, :] = m_new

    def step(j, q_slot_next, j_next, s_cur, s_next, masked):
        k_next = k_ref[block(j_next), :]
        v = v_ref[block(j), :]
        v_ones = jnp.concatenate([v, jnp.ones_like(v)], axis=1)
        for r0 in range(0, 2 * tq, rc):
            if s_next is s_cur:
                update_chunk(s_cur, v_ones, r0, masked)
                scores_chunk(q_slot_next, k_next, s_next, r0)
            else:
                scores_chunk(q_slot_next, k_next, s_next, r0)
                update_chunk(s_cur, v_ones, r0, masked)

    def finish(i):
        lam = (jnp.exp(jnp.sum(lq1_ref[...] * lk1_ref[...], axis=-1, keepdims=True))
               - jnp.exp(jnp.sum(lq2_ref[...] * lk2_ref[...], axis=-1, keepdims=True))
               + lambda_init)
        o_all = acc_ref[...] / l_ref[...]
        o = o_all[0:tq] - lam * o_all[tq:2 * tq]
        o_ref[block(i), :] = (_rms(o, sw_ref[...]) * (1.0 - lambda_init)).astype(o_ref.dtype)

    def straight_steps(i, j, count):
        for u in range(0, count, 2):
            step(j + u, i % 2, j + u + 1, sa_ref, sb_ref, masked=False)
            step(j + u + 1, i % 2, j + u + 2, sb_ref, sa_ref, masked=False)

    def q_block(i, carry):
        m_ref[...] = jnp.full_like(m_ref, -jnp.inf)
        l_ref[...] = jnp.zeros_like(l_ref)
        acc_ref[...] = jnp.zeros_like(acc_ref)

        @pl.when(i + 1 < nq)
        def _():
            prep_q(i + 1)

        def quad(g, c):
            straight_steps(i, 4 * g, 4)
            return c

        n_quads = i // 4
        lax.fori_loop(0, n_quads, quad, 0)
        rest = i - 4 * n_quads
        q_slot_after = jnp.minimum(i + 1, nq - 1) % 2

        @pl.when(rest >= 2)
        def _():
            straight_steps(i, 4 * n_quads, 2)

        @pl.when(rest % 2 == 1)
        def _():
            step(i - 1, i % 2, i, sa_ref, sb_ref, masked=False)
            step(i, q_slot_after, 0, sb_ref, sa_ref, masked=True)
            finish(i)

        @pl.when(rest % 2 == 0)
        def _():
            step(i, q_slot_after, 0, sa_ref, sa_ref, masked=True)
            finish(i)

        return carry

    prep_q(0)
    k0 = k_ref[block(0), :]
    for r0 in range(0, 2 * tq, rc):
        scores_chunk(0, k0, sa_ref, r0)
    lax.fori_loop(0, nq, q_block, 0)


def _attn(proj, lq1, lk1, lq2, lk2, subln, batch, seq, lambda_init, cast_weights, *, tq=512, rc=256):
    n_tok = proj.shape[0]
    q_blk = 3 * (N_HEADS * DV) // LANES
    k_blk = q_blk + N_HEADS
    v_blk = k_blk + N_HEADS
    vec = lambda width: pl.BlockSpec((1, width), lambda b, h: (0, 0))
    head = lambda first_blk: pl.BlockSpec((seq, LANES), lambda b, h: (b, first_blk + h))
    n_grid = batch * N_HEADS
    cast_in, cast_out, cast_shapes = [], [], []
    for w, tile in cast_weights:
        rows, cols = w.shape
        assert rows % (n_grid * BF16_ROWS) == 0, w.shape
        slab_rows = rows // n_grid
        cast_in.append(pl.BlockSpec((slab_rows, cols), lambda b, h: (b * N_HEADS + h, 0)))
        if tile is None:
            cast_out.append(pl.BlockSpec((slab_rows, cols), lambda b, h: (b * N_HEADS + h, 0)))
            cast_shapes.append(jax.ShapeDtypeStruct((rows, cols), BF16))
        else:
            cast_out.append(pl.BlockSpec((cols // tile, slab_rows, tile), lambda b, h: (0, b * N_HEADS + h, 0)))
            cast_shapes.append(jax.ShapeDtypeStruct((cols // tile, rows, tile), BF16))
    outs = pl.pallas_call(
        functools.partial(_attn_kernel, n_cast=len(cast_weights), tq=tq, rc=rc, lambda_init=lambda_init),
        grid=(batch, N_HEADS),
        in_specs=[vec(DK), vec(DK), vec(DK), vec(DK), vec(DV), head(q_blk), head(k_blk), head(v_blk)] + cast_in,
        out_specs=[head(0)] + cast_out,
        out_shape=[jax.ShapeDtypeStruct((n_tok, N_HEADS * DV), BF16)] + cast_shapes,
        scratch_shapes=[
            pltpu.VMEM((2, 2 * tq, LANES), BF16),
            pltpu.VMEM((2 * tq, tq), F32),
            pltpu.VMEM((2 * tq, tq), F32),
            pltpu.VMEM((2 * tq, LANES), F32),
            pltpu.VMEM((2 * tq, LANES), F32),
            pltpu.VMEM((2 * tq, DV), F32),
        ],
        compiler_params=pltpu.CompilerParams(
            dimension_semantics=("arbitrary", "arbitrary"),
            vmem_limit_bytes=48 * MIB),
        name="diff_attn",
    )(lq1, lk1, lq2, lk2, subln, proj, proj, proj, *[w for w, _ in cast_weights])
    return outs[0], outs[1:]


def _conv_taps(cx_m2, cx_m1, cx, w):
    return w[0:1, :] * cx_m2 + w[1:2, :] * cx_m1 + w[2:3, :] * cx


def _out_proj_kernel(x_ref, b_ref, c_ref, xv_ref, ch_ref, xh_ref, cw_ref, ya_ref, wc_ref, wa_ref, o_ref,
                     yc_ref, *, tiles_per_seq):
    h = BF16_ROWS
    at_seq_start = pl.program_id(0) % tiles_per_seq == 0
    row = lax.broadcasted_iota(jnp.int32, (h, 1), 0)
    for c0 in range(0, cw_ref.shape[1], CONV_COLS):
        cols = slice(c0, c0 + CONV_COLS)
        w = cw_ref[:, cols]
        cx = c_ref[:, cols].astype(F32) * xv_ref[:, cols].astype(F32)
        y = _conv_taps(pltpu.roll(cx, 2, axis=0), pltpu.roll(cx, 1, axis=0), cx, w)
        yc_ref[:, cols] = (b_ref[:, cols].astype(F32) * y).astype(BF16)
        halo = ch_ref[:, cols].astype(F32) * xh_ref[:, cols].astype(F32)
        halo = jnp.where(at_seq_start, 0.0, halo)
        top = cx[:h]
        m1 = jnp.where(row < 1, pltpu.roll(halo, 1, axis=0), pltpu.roll(top, 1, axis=0))
        m2 = jnp.where(row < 2, pltpu.roll(halo, 2, axis=0), pltpu.roll(top, 2, axis=0))
        yc_ref[0:h, cols] = (b_ref[0:h, cols].astype(F32) * _conv_taps(m2, m1, top, w)).astype(BF16)
    y = jnp.dot(ya_ref[...], wa_ref[...], preferred_element_type=F32)
    y += jnp.dot(yc_ref[...], wc_ref[...], preferred_element_type=F32)
    o_ref[...] = x_ref[...] + y


def _out_proj(x, proj, y_attn, conv_w, w_out, seq, *, tm=512):
    n_tok, d = x.shape
    cw = conv_w.shape[1]
    aw = y_attn.shape[1]
    halo_blocks = tm // BF16_ROWS

    def halo_map(sec):
        return lambda i: (jnp.maximum(i * halo_blocks - 1, 0), sec)

    resident = dict(pipeline_mode=pl.Buffered(1))
    return pl.pallas_call(
        functools.partial(_out_proj_kernel, tiles_per_seq=seq // tm),
        grid=(n_tok // tm,),
        in_specs=[
            pl.BlockSpec((tm, d), lambda i: (i, 0)),
            pl.BlockSpec((tm, cw), lambda i: (i, 0)),
            pl.BlockSpec((tm, cw), lambda i: (i, 1)),
            pl.BlockSpec((tm, cw), lambda i: (i, 2)),
            pl.BlockSpec((BF16_ROWS, cw), halo_map(1)),
            pl.BlockSpec((BF16_ROWS, cw), halo_map(2)),
            pl.BlockSpec((CONV_K, cw), lambda i: (0, 0)),
            pl.BlockSpec((tm, aw), lambda i: (i, 0)),
            pl.BlockSpec((cw, d), lambda i: (0, 0), **resident),
            pl.BlockSpec((aw, d), lambda i: (1, 0), **resident),
        ],
        out_specs=pl.BlockSpec((tm, d), lambda i: (i, 0)),
        out_shape=jax.ShapeDtypeStruct((n_tok, d), F32),
        scratch_shapes=[pltpu.VMEM((tm, cw), BF16)],
        compiler_params=pltpu.CompilerParams(
            dimension_semantics=("arbitrary",),
            vmem_limit_bytes=48 * MIB),
        name="out_proj",
    )(x, proj, proj, proj, proj, proj, conv_w, y_attn, w_out, w_out)


def kernel(x, ffn1_norm, ffn1_w_gate, ffn1_w_up, ffn1_w_down, mix_norm, w_in, conv_w, lambda_q1, lambda_k1, lambda_q2, lambda_k2, subln_w, w_out, ffn2_norm, ffn2_w_gate, ffn2_w_up, ffn2_w_down, final_norm):
    batch, seq, d = x.shape
    depth = ffn1_norm.shape[0]
    assert w_in.shape[2] == 3 * conv_w.shape[2] + 3 * N_HEADS * DV
    assert conv_w.shape[2] == N_HEADS * DV
    h = x.reshape(batch * seq, d)
    row = lambda a: a.reshape(1, -1)
    for l in range(depth):
        last = l == depth - 1
        lambda_init = 0.8 - 0.6 * math.exp(-0.3 * l)
        h = _ffn(h, row(ffn1_norm[l]), _column_tiles(ffn1_w_gate[l].astype(BF16), FFN_COLS),
                 _column_tiles(ffn1_w_up[l].astype(BF16), FFN_COLS), ffn1_w_down[l].astype(BF16),
                 row(final_norm), final_norm=False)
        proj = _in_proj(h, row(mix_norm[l]), w_in[l].astype(BF16))
        y_attn, (w_out_b, wg2, wu2, wd2) = _attn(
            proj, row(lambda_q1[l]), row(lambda_k1[l]), row(lambda_q2[l]), row(lambda_k2[l]), row(subln_w[l]),
            batch, seq, lambda_init,
            [(w_out[l], None), (ffn2_w_gate[l], FFN_COLS), (ffn2_w_up[l], FFN_COLS), (ffn2_w_down[l], None)])
        h = _out_proj(h, proj, y_attn, conv_w[l], w_out_b, seq)
        h = _ffn(h, row(ffn2_norm[l]), wg2, wu2, wd2, row(final_norm), final_norm=last)
    if depth == 0:
        raise ValueError("depth must be at least 1")
    return h.reshape(batch, seq, d)
```

```python
import functools
import math

import jax
import jax.numpy as jnp
from jax import lax
from jax.experimental import pallas as pl
from jax.experimental.pallas import tpu as pltpu

EPS = 1e-6
CONV_K = 3
N_HEADS = 8
DK = 64
DV = 128
LANES = 128
BF16_ROWS = 16
FFN_ROWS = 256
CONV_COLS = 256
MIB = 1024 * 1024

F32 = jnp.float32
BF16 = jnp.bfloat16


def _rms(x, g):
    return x * lax.rsqrt(jnp.mean(x * x, axis=-1, keepdims=True) + EPS) * g


def _ffn_kernel(x_ref, g_ref, wg_ref, wu_ref, wd_ref, fg_ref, o_ref, n_ref, *, final_norm):
    f = pl.program_id(1)
    last = pl.num_programs(1) - 1
    chunks = [slice(r0, r0 + FFN_ROWS) for r0 in range(0, x_ref.shape[0], FFN_ROWS)]

    def swiglu_down(n):
        gate = jnp.dot(n, wg_ref[...], preferred_element_type=F32)
        up = jnp.dot(n, wu_ref[...], preferred_element_type=F32)
        h = (gate * jax.nn.sigmoid(gate) * up).astype(BF16)
        return jnp.dot(h, wd_ref[...], preferred_element_type=F32)

    @pl.when(f == 0)
    def _():
        for rows in chunks:
            n = _rms(x_ref[rows, :], g_ref[...]).astype(BF16)
            n_ref[rows, :] = n
            o_ref[rows, :] = swiglu_down(n)

    @pl.when(jnp.logical_and(f > 0, f < last))
    def _():
        o_ref[...] += swiglu_down(n_ref[...])

    @pl.when(f == last)
    def _():
        for rows in chunks:
            y = x_ref[rows, :] + 0.5 * (o_ref[rows, :] + swiglu_down(n_ref[rows, :]))
            if final_norm:
                y = _rms(y, fg_ref[...])
            o_ref[rows, :] = y


def _ffn(x, gain, wg, wu, wd, final_gain, *, final_norm, tm=1024, tf=512):
    n_tok, d = x.shape
    ffn = wg.shape[1]
    assert ffn // tf >= 2
    return pl.pallas_call(
        functools.partial(_ffn_kernel, final_norm=final_norm),
        grid=(n_tok // tm, ffn // tf),
        in_specs=[
            pl.BlockSpec((tm, d), lambda i, f: (i, 0)),
            pl.BlockSpec((1, d), lambda i, f: (0, 0)),
            pl.BlockSpec((d, tf), lambda i, f: (0, f)),
            pl.BlockSpec((d, tf), lambda i, f: (0, f)),
            pl.BlockSpec((tf, d), lambda i, f: (f, 0)),
            pl.BlockSpec((1, d), lambda i, f: (0, 0)),
        ],
        out_specs=pl.BlockSpec((tm, d), lambda i, f: (i, 0)),
        out_shape=jax.ShapeDtypeStruct((n_tok, d), F32),
        scratch_shapes=[pltpu.VMEM((tm, d), BF16)],
        compiler_params=pltpu.CompilerParams(
            dimension_semantics=("arbitrary", "arbitrary"),
            vmem_limit_bytes=56 * MIB),
        name="ffn",
    )(x, gain, wg, wu, wd, final_gain)


def _in_proj_kernel(x_ref, g_ref, w_ref, o_ref, n_ref):
    j = pl.program_id(1)

    @pl.when(j == 0)
    def _():
        for r0 in range(0, x_ref.shape[0], FFN_ROWS):
            rows = slice(r0, r0 + FFN_ROWS)
            n = _rms(x_ref[rows, :], g_ref[...]).astype(BF16)
            n_ref[rows, :] = n
            o_ref[rows, :] = jnp.dot(n, w_ref[...], preferred_element_type=F32).astype(o_ref.dtype)

    @pl.when(j > 0)
    def _():
        o_ref[...] = jnp.dot(n_ref[...], w_ref[...], preferred_element_type=F32).astype(o_ref.dtype)


def _in_proj(x, gain, w, *, tm=1024, tn=2048):
    n_tok, d = x.shape
    cols = w.shape[1]
    return pl.pallas_call(
        _in_proj_kernel,
        grid=(n_tok // tm, cols // tn),
        in_specs=[
            pl.BlockSpec((tm, d), lambda i, j: (i, 0)),
            pl.BlockSpec((1, d), lambda i, j: (0, 0)),
            pl.BlockSpec((d, tn), lambda i, j: (0, j)),
        ],
        out_specs=pl.BlockSpec((tm, tn), lambda i, j: (i, j)),
        out_shape=jax.ShapeDtypeStruct((n_tok, cols), BF16),
        scratch_shapes=[pltpu.VMEM((tm, d), BF16)],
        compiler_params=pltpu.CompilerParams(
            dimension_semantics=("arbitrary", "arbitrary"),
            vmem_limit_bytes=48 * MIB),
        name="in_proj",
    )(x, gain, w)


def _attn_kernel(*refs, n_cast, tq, rc, lambda_init):
    lq1_ref, lk1_ref, lq2_ref, lk2_ref, sw_ref, q_ref, k_ref, v_ref = refs[:8]
    o_ref = refs[8 + n_cast]
    q2_ref, sa_ref, sb_ref, m_ref, l_ref, acc_ref = refs[9 + 2 * n_cast:]
    for src_ref, dst_ref in zip(refs[8:8 + n_cast], refs[9 + n_cast:9 + 2 * n_cast]):
        dst_ref[...] = src_ref[...].astype(BF16)

    seq = q_ref.shape[0]
    nq = seq // tq
    scale = 1.0 / math.sqrt(DK)
    contract_last = (((1,), (1,)), ((), ()))

    def block(idx):
        return pl.ds(pl.multiple_of(idx * tq, tq), tq)

    def prep_q(i):
        q = q_ref[block(i), :]
        lane = lax.broadcasted_iota(jnp.int32, q.shape, 1)
        qs = (q.astype(F32) * scale).astype(BF16)
        zero = jnp.zeros_like(qs)
        q2_ref[i % 2, 0:tq, :] = jnp.where(lane < DK, qs, zero)
        q2_ref[i % 2, tq:2 * tq, :] = jnp.where(lane >= DK, qs, zero)

    def scores_chunk(q_slot, k, s_ref, r0):
        s_ref[r0:r0 + rc, :] = lax.dot_general(
            q2_ref[q_slot, r0:r0 + rc, :], k, contract_last, preferred_element_type=F32)

    def update_chunk(s_ref, v_ones, r0, masked):
        s = s_ref[r0:r0 + rc, :]
        if masked:
            q_pos = lax.broadcasted_iota(jnp.int32, (rc, tq), 0) + (r0 % tq)
            k_pos = lax.broadcasted_iota(jnp.int32, (rc, tq), 1)
            s = jnp.where(k_pos <= q_pos, s, -jnp.inf)
        cols = [s[:, c0:c0 + LANES] for c0 in range(0, tq, LANES)]
        lane_max = functools.reduce(jnp.maximum, cols)
        row_max = jnp.broadcast_to(jnp.max(lane_max, axis=-1, keepdims=True), (rc, LANES))
        rows = pl.ds(r0, rc)
        m_prev = m_ref[rows, :]
        m_new = jnp.maximum(m_prev, row_max)
        alpha = jnp.exp(m_prev - m_new)
        p = jnp.concatenate([jnp.exp(col - m_new) for col in cols], axis=1).astype(BF16)
        pv = jnp.dot(p, v_ones, preferred_element_type=F32)
        acc_ref[rows, :] = alpha * acc_ref[rows, :] + pv[:, :DV]
        l_ref[rows, :] = alpha * l_ref[rows, :] + pv[:, DV:]
        m_ref[rows, :] = m_new

    def step(j, q_slot_next, j_next, s_cur, s_next, masked):
        k_next = k_ref[block(j_next), :]
        v = v_ref[block(j), :]
        v_ones = jnp.concatenate([v, jnp.ones_like(v)], axis=1)
        for r0 in range(0, 2 * tq, rc):
            if s_next is s_cur:
                update_chunk(s_cur, v_ones, r0, masked)
                scores_chunk(q_slot_next, k_next, s_next, r0)
            else:
                scores_chunk(q_slot_next, k_next, s_next, r0)
                update_chunk(s_cur, v_ones, r0, masked)

    def finish(i):
        lam = (jnp.exp(jnp.sum(lq1_ref[...] * lk1_ref[...], axis=-1, keepdims=True))
               - jnp.exp(jnp.sum(lq2_ref[...] * lk2_ref[...], axis=-1, keepdims=True))
               + lambda_init)
        o_all = acc_ref[...] / l_ref[...]
        o = o_all[0:tq] - lam * o_all[tq:2 * tq]
        o_ref[block(i), :] = (_rms(o, sw_ref[...]) * (1.0 - lambda_init)).astype(o_ref.dtype)

    def straight_steps(i, j, count):
        for u in range(0, count, 2):
            step(j + u, i % 2, j + u + 1, sa_ref, sb_ref, masked=False)
            step(j + u + 1, i % 2, j + u + 2, sb_ref, sa_ref, masked=False)

    def q_block(i, carry):
        m_ref[...] = jnp.full_like(m_ref, -jnp.inf)
        l_ref[...] = jnp.zeros_like(l_ref)
        acc_ref[...] = jnp.zeros_like(acc_ref)

        @pl.when(i + 1 < nq)
        def _():
            prep_q(i + 1)

        def octet(g, c):
            straight_steps(i, 8 * g, 8)
            return c

        n_octets = i // 8
        lax.fori_loop(0, n_octets, octet, 0)
        n_quads = i // 4

        @pl.when(n_quads % 2 == 1)
        def _():
            straight_steps(i, 8 * n_octets, 4)

        rest = i - 4 * n_quads
        q_slot_after = jnp.minimum(i + 1, nq - 1) % 2

        for tail in range(4):
            @pl.when(rest == tail)
            def _(tail=tail):
                straight_steps(i, 4 * n_quads, tail - tail % 2)
                if tail % 2 == 1:
                    step(i - 1, i % 2, i, sa_ref, sb_ref, masked=False)
                    step(i, q_slot_after, 0, sb_ref, sa_ref, masked=True)
                else:
                    step(i, q_slot_after, 0, sa_ref, sa_ref, masked=True)
                finish(i)

        return carry

    prep_q(0)
    k0 = k_ref[block(0), :]
    for r0 in range(0, 2 * tq, rc):
        scores_chunk(0, k0, sa_ref, r0)
    lax.fori_loop(0, nq, q_block, 0)


def _attn(proj, lq1, lk1, lq2, lk2, subln, batch, seq, lambda_init, cast_weights, *, tq=512, rc=256):
    n_tok = proj.shape[0]
    q_blk = 3 * (N_HEADS * DV) // LANES
    k_blk = q_blk + N_HEADS
    v_blk = k_blk + N_HEADS
    vec = lambda width: pl.BlockSpec((1, width), lambda b, h: (0, 0))
    head = lambda first_blk: pl.BlockSpec((seq, LANES), lambda b, h: (b, first_blk + h))
    n_grid = batch * N_HEADS
    for w in cast_weights:
        assert w.shape[0] % (n_grid * BF16_ROWS) == 0, w.shape
    slab = lambda w: pl.BlockSpec((w.shape[0] // n_grid, w.shape[1]), lambda b, h: (b * N_HEADS + h, 0))
    outs = pl.pallas_call(
        functools.partial(_attn_kernel, n_cast=len(cast_weights), tq=tq, rc=rc, lambda_init=lambda_init),
        grid=(batch, N_HEADS),
        in_specs=[vec(DK), vec(DK), vec(DK), vec(DK), vec(DV), head(q_blk), head(k_blk), head(v_blk)]
        + [slab(w) for w in cast_weights],
        out_specs=[head(0)] + [slab(w) for w in cast_weights],
        out_shape=[jax.ShapeDtypeStruct((n_tok, N_HEADS * DV), BF16)]
        + [jax.ShapeDtypeStruct(w.shape, BF16) for w in cast_weights],
        scratch_shapes=[
            pltpu.VMEM((2, 2 * tq, LANES), BF16),
            pltpu.VMEM((2 * tq, tq), F32),
            pltpu.VMEM((2 * tq, tq), F32),
            pltpu.VMEM((2 * tq, LANES), F32),
            pltpu.VMEM((2 * tq, LANES), F32),
            pltpu.VMEM((2 * tq, DV), F32),
        ],
        compiler_params=pltpu.CompilerParams(
            dimension_semantics=("arbitrary", "arbitrary"),
            vmem_limit_bytes=48 * MIB),
        name="diff_attn",
    )(lq1, lk1, lq2, lk2, subln, proj, proj, proj, *cast_weights)
    return outs[0], outs[1:]


def _conv_taps(cx_m2, cx_m1, cx, w):
    return w[0:1, :] * cx_m2 + w[1:2, :] * cx_m1 + w[2:3, :] * cx


def _out_proj_kernel(x_ref, b_ref, c_ref, xv_ref, ch_ref, xh_ref, cw_ref, ya_ref, wc_ref, wa_ref, o_ref,
                     yc_ref, *, tiles_per_seq):
    h = BF16_ROWS
    at_seq_start = pl.program_id(0) % tiles_per_seq == 0
    row = lax.broadcasted_iota(jnp.int32, (h, 1), 0)
    for c0 in range(0, cw_ref.shape[1], CONV_COLS):
        cols = slice(c0, c0 + CONV_COLS)
        w = cw_ref[:, cols]
        cx = c_ref[:, cols].astype(F32) * xv_ref[:, cols].astype(F32)
        y = _conv_taps(pltpu.roll(cx, 2, axis=0), pltpu.roll(cx, 1, axis=0), cx, w)
        yc_ref[:, cols] = (b_ref[:, cols].astype(F32) * y).astype(BF16)
        halo = ch_ref[:, cols].astype(F32) * xh_ref[:, cols].astype(F32)
        halo = jnp.where(at_seq_start, 0.0, halo)
        top = cx[:h]
        m1 = jnp.where(row < 1, pltpu.roll(halo, 1, axis=0), pltpu.roll(top, 1, axis=0))
        m2 = jnp.where(row < 2, pltpu.roll(halo, 2, axis=0), pltpu.roll(top, 2, axis=0))
        yc_ref[0:h, cols] = (b_ref[0:h, cols].astype(F32) * _conv_taps(m2, m1, top, w)).astype(BF16)
    y = jnp.dot(ya_ref[...], wa_ref[...], preferred_element_type=F32)
    y += jnp.dot(yc_ref[...], wc_ref[...], preferred_element_type=F32)
    o_ref[...] = x_ref[...] + y


def _out_proj(x, proj, y_attn, conv_w, w_out, seq, *, tm=512):
    n_tok, d = x.shape
    cw = conv_w.shape[1]
    aw = y_attn.shape[1]
    halo_blocks = tm // BF16_ROWS

    def halo_map(sec):
        return lambda i: (jnp.maximum(i * halo_blocks - 1, 0), sec)

    resident = dict(pipeline_mode=pl.Buffered(1))
    return pl.pallas_call(
        functools.partial(_out_proj_kernel, tiles_per_seq=seq // tm),
        grid=(n_tok // tm,),
        in_specs=[
            pl.BlockSpec((tm, d), lambda i: (i, 0)),
            pl.BlockSpec((tm, cw), lambda i: (i, 0)),
            pl.BlockSpec((tm, cw), lambda i: (i, 1)),
            pl.BlockSpec((tm, cw), lambda i: (i, 2)),
            pl.BlockSpec((BF16_ROWS, cw), halo_map(1)),
            pl.BlockSpec((BF16_ROWS, cw), halo_map(2)),
            pl.BlockSpec((CONV_K, cw), lambda i: (0, 0)),
            pl.BlockSpec((tm, aw), lambda i: (i, 0)),
            pl.BlockSpec((cw, d), lambda i: (0, 0), **resident),
            pl.BlockSpec((aw, d), lambda i: (1, 0), **resident),
        ],
        out_specs=pl.BlockSpec((tm, d), lambda i: (i, 0)),
        out_shape=jax.ShapeDtypeStruct((n_tok, d), F32),
        scratch_shapes=[pltpu.VMEM((tm, cw), BF16)],
        compiler_params=pltpu.CompilerParams(
            dimension_semantics=("arbitrary",),
            vmem_limit_bytes=48 * MIB),
        name="out_proj",
    )(x, proj, proj, proj, proj, proj, conv_w, y_attn, w_out, w_out)


def kernel(x, ffn1_norm, ffn1_w_gate, ffn1_w_up, ffn1_w_down, mix_norm, w_in, conv_w, lambda_q1, lambda_k1, lambda_q2, lambda_k2, subln_w, w_out, ffn2_norm, ffn2_w_gate, ffn2_w_up, ffn2_w_down, final_norm):
    batch, seq, d = x.shape
    depth = ffn1_norm.shape[0]
    assert w_in.shape[2] == 3 * conv_w.shape[2] + 3 * N_HEADS * DV
    assert conv_w.shape[2] == N_HEADS * DV
    h = x.reshape(batch * seq, d)
    row = lambda a: a.reshape(1, -1)
    for l in range(depth):
        last = l == depth - 1
        lambda_init = 0.8 - 0.6 * math.exp(-0.3 * l)
        h = _ffn(h, row(ffn1_norm[l]), ffn1_w_gate[l].astype(BF16), ffn1_w_up[l].astype(BF16),
                 ffn1_w_down[l].astype(BF16), row(final_norm), final_norm=False)
        proj = _in_proj(h, row(mix_norm[l]), w_in[l].astype(BF16))
        y_attn, (w_out_b, wg2, wu2, wd2) = _attn(
            proj, row(lambda_q1[l]), row(lambda_k1[l]), row(lambda_q2[l]), row(lambda_k2[l]), row(subln_w[l]),
            batch, seq, lambda_init, [w_out[l], ffn2_w_gate[l], ffn2_w_up[l], ffn2_w_down[l]])
        h = _out_proj(h, proj, y_attn, conv_w[l], w_out_b, seq)
        h = _ffn(h, row(ffn2_norm[l]), wg2, wu2, wd2, row(final_norm), final_norm=last)
    if depth == 0:
        raise ValueError("depth must be at least 1")
    return h.reshape(batch, seq, d)
```

```python
import functools
import math

import jax
import jax.numpy as jnp
from jax import lax
from jax.experimental import pallas as pl
from jax.experimental.pallas import tpu as pltpu

EPS = 1e-6
CONV_K = 3
N_HEADS = 8
DK = 64
DV = 128
LANES = 128
BF16_ROWS = 16
FFN_ROWS = 512
PROJ_ROWS = 256
CONV_COLS = 256
MIB = 1024 * 1024

F32 = jnp.float32
BF16 = jnp.bfloat16


def _rms(x, g):
    return x * lax.rsqrt(jnp.mean(x * x, axis=-1, keepdims=True) + EPS) * g


def _ffn_kernel(x_ref, g_ref, wg_ref, wu_ref, wd_ref, fg_ref, o_ref, n_ref, *, final_norm):
    f = pl.program_id(1)
    last = pl.num_programs(1) - 1
    chunks = [slice(r0, r0 + FFN_ROWS) for r0 in range(0, x_ref.shape[0], FFN_ROWS)]

    def swiglu_down(n):
        gate = jnp.dot(n, wg_ref[...], preferred_element_type=F32)
        up = jnp.dot(n, wu_ref[...], preferred_element_type=F32)
        h = (gate * jax.nn.sigmoid(gate) * up).astype(BF16)
        return jnp.dot(h, wd_ref[...], preferred_element_type=F32)

    @pl.when(f == 0)
    def _():
        for rows in chunks:
            n = _rms(x_ref[rows, :], g_ref[...]).astype(BF16)
            n_ref[rows, :] = n
            o_ref[rows, :] = swiglu_down(n)

    @pl.when(jnp.logical_and(f > 0, f < last))
    def _():
        o_ref[...] += swiglu_down(n_ref[...])

    @pl.when(f == last)
    def _():
        for rows in chunks:
            y = x_ref[rows, :] + 0.5 * (o_ref[rows, :] + swiglu_down(n_ref[rows, :]))
            if final_norm:
                y = _rms(y, fg_ref[...])
            o_ref[rows, :] = y


def _ffn(x, gain, wg, wu, wd, final_gain, *, final_norm, tm=1024, tf=512):
    n_tok, d = x.shape
    ffn = wg.shape[1]
    assert ffn // tf >= 2
    return pl.pallas_call(
        functools.partial(_ffn_kernel, final_norm=final_norm),
        grid=(n_tok // tm, ffn // tf),
        in_specs=[
            pl.BlockSpec((tm, d), lambda i, f: (i, 0)),
            pl.BlockSpec((1, d), lambda i, f: (0, 0)),
            pl.BlockSpec((d, tf), lambda i, f: (0, f)),
            pl.BlockSpec((d, tf), lambda i, f: (0, f)),
            pl.BlockSpec((tf, d), lambda i, f: (f, 0)),
            pl.BlockSpec((1, d), lambda i, f: (0, 0)),
        ],
        out_specs=pl.BlockSpec((tm, d), lambda i, f: (i, 0)),
        out_shape=jax.ShapeDtypeStruct((n_tok, d), F32),
        scratch_shapes=[pltpu.VMEM((tm, d), BF16)],
        compiler_params=pltpu.CompilerParams(
            dimension_semantics=("arbitrary", "arbitrary"),
            vmem_limit_bytes=56 * MIB),
        name="ffn",
    )(x, gain, wg, wu, wd, final_gain)


def _in_proj_kernel(x_ref, g_ref, w_ref, o_ref, n_ref):
    j = pl.program_id(1)

    @pl.when(j == 0)
    def _():
        for r0 in range(0, x_ref.shape[0], PROJ_ROWS):
            rows = slice(r0, r0 + PROJ_ROWS)
            n = _rms(x_ref[rows, :], g_ref[...]).astype(BF16)
            n_ref[rows, :] = n
            o_ref[rows, :] = jnp.dot(n, w_ref[...], preferred_element_type=F32).astype(o_ref.dtype)

    @pl.when(j > 0)
    def _():
        o_ref[...] = jnp.dot(n_ref[...], w_ref[...], preferred_element_type=F32).astype(o_ref.dtype)


def _in_proj(x, gain, w, *, tm=1024, tn=2048):
    n_tok, d = x.shape
    cols = w.shape[1]
    return pl.pallas_call(
        _in_proj_kernel,
        grid=(n_tok // tm, cols // tn),
        in_specs=[
            pl.BlockSpec((tm, d), lambda i, j: (i, 0)),
            pl.BlockSpec((1, d), lambda i, j: (0, 0)),
            pl.BlockSpec((d, tn), lambda i, j: (0, j)),
        ],
        out_specs=pl.BlockSpec((tm, tn), lambda i, j: (i, j)),
        out_shape=jax.ShapeDtypeStruct((n_tok, cols), BF16),
        scratch_shapes=[pltpu.VMEM((tm, d), BF16)],
        compiler_params=pltpu.CompilerParams(
            dimension_semantics=("arbitrary", "arbitrary"),
            vmem_limit_bytes=48 * MIB),
        name="in_proj",
    )(x, gain, w)


def _attn_kernel(*refs, n_cast, tq, rc, lambda_init):
    lq1_ref, lk1_ref, lq2_ref, lk2_ref, sw_ref, q_ref, k_ref, v_ref = refs[:8]
    o_ref = refs[8 + n_cast]
    q2_ref, sa_ref, sb_ref, m_ref, l_ref, acc_ref = refs[9 + 2 * n_cast:]
    for src_ref, dst_ref in zip(refs[8:8 + n_cast], refs[9 + n_cast:9 + 2 * n_cast]):
        dst_ref[...] = src_ref[...].astype(BF16)

    seq = q_ref.shape[0]
    nq = seq // tq
    scale = 1.0 / math.sqrt(DK)
    contract_last = (((1,), (1,)), ((), ()))

    def block(idx):
        return pl.ds(pl.multiple_of(idx * tq, tq), tq)

    def prep_q(i, slot):
        q = q_ref[block(i), :]
        lane = lax.broadcasted_iota(jnp.int32, q.shape, 1)
        qs = (q.astype(F32) * scale).astype(BF16)
        zero = jnp.zeros_like(qs)
        q2_ref[slot, 0:tq, :] = jnp.where(lane < DK, qs, zero)
        q2_ref[slot, tq:2 * tq, :] = jnp.where(lane >= DK, qs, zero)

    def reset_stats():
        m_ref[...] = jnp.full_like(m_ref, -jnp.inf)
        l_ref[...] = jnp.zeros_like(l_ref)
        acc_ref[...] = jnp.zeros_like(acc_ref)

    def scores_chunk(q_slot, k, s_ref, r0):
        s_ref[r0:r0 + rc, :] = lax.dot_general(
            q2_ref[q_slot, r0:r0 + rc, :], k, contract_last, preferred_element_type=F32)

    def update_chunk(s_ref, v_ones, r0, masked):
        s = s_ref[r0:r0 + rc, :]
        if masked:
            q_pos = lax.broadcasted_iota(jnp.int32, (rc, tq), 0) + (r0 % tq)
            k_pos = lax.broadcasted_iota(jnp.int32, (rc, tq), 1)
            s = jnp.where(k_pos <= q_pos, s, -jnp.inf)
        cols = [s[:, c0:c0 + LANES] for c0 in range(0, tq, LANES)]
        lane_max = functools.reduce(jnp.maximum, cols)
        row_max = jnp.broadcast_to(jnp.max(lane_max, axis=-1, keepdims=True), (rc, LANES))
        rows = pl.ds(r0, rc)
        m_prev = m_ref[rows, :]
        m_new = jnp.maximum(m_prev, row_max)
        alpha = jnp.exp(m_prev - m_new)
        p = jnp.concatenate([jnp.exp(col - m_new) for col in cols], axis=1).astype(BF16)
        pv = jnp.dot(p, v_ones, preferred_element_type=F32)
        acc_ref[rows, :] = alpha * acc_ref[rows, :] + pv[:, :DV]
        l_ref[rows, :] = alpha * l_ref[rows, :] + pv[:, DV:]
        m_ref[rows, :] = m_new

    def step(j, q_slot_next, j_next, s_cur, s_next, masked):
        k_next = k_ref[block(j_next), :]
        v = v_ref[block(j), :]
        v_ones = jnp.concatenate([v, jnp.ones_like(v)], axis=1)
        for r0 in range(0, 2 * tq, rc):
            if s_next is s_cur:
                update_chunk(s_cur, v_ones, r0, masked)
                scores_chunk(q_slot_next, k_next, s_next, r0)
            else:
                scores_chunk(q_slot_next, k_next, s_next, r0)
                update_chunk(s_cur, v_ones, r0, masked)

    def finish(i):
        lam = (jnp.exp(jnp.sum(lq1_ref[...] * lk1_ref[...], axis=-1, keepdims=True))
               - jnp.exp(jnp.sum(lq2_ref[...] * lk2_ref[...], axis=-1, keepdims=True))
               + lambda_init)
        o_all = acc_ref[...] / l_ref[...]
        o = o_all[0:tq] - lam * o_all[tq:2 * tq]
        o_ref[block(i), :] = (_rms(o, sw_ref[...]) * (1.0 - lambda_init)).astype(o_ref.dtype)

    def straight_steps(i, j, count):
        for u in range(0, count, 2):
            step(j + u, i % 2, j + u + 1, sa_ref, sb_ref, masked=False)
            step(j + u + 1, i % 2, j + u + 2, sb_ref, sa_ref, masked=False)

    def q_block(i, carry):
        def octet(g, c):
            straight_steps(i, 8 * g, 8)
            return c

        n_octets = i // 8
        lax.fori_loop(0, n_octets, octet, 0)
        n_quads = i // 4

        @pl.when(n_quads % 2 == 1)
        def _():
            straight_steps(i, 8 * n_octets, 4)

        rest = i - 4 * n_quads
        q_slot_after = jnp.minimum(i + 1, nq - 1) % 2

        for tail in range(4):
            @pl.when(rest == tail)
            def _(tail=tail):
                straight_steps(i, 4 * n_quads, tail - tail % 2)
                prep_q(jnp.minimum(i + 1, nq - 1), (i + 1) % 2)
                if tail % 2 == 1:
                    step(i - 1, i % 2, i, sa_ref, sb_ref, masked=False)
                    step(i, q_slot_after, 0, sb_ref, sa_ref, masked=True)
                else:
                    step(i, q_slot_after, 0, sa_ref, sa_ref, masked=True)
                finish(i)
                reset_stats()

        return carry

    prep_q(0, 0)
    reset_stats()
    k0 = k_ref[block(0), :]
    for r0 in range(0, 2 * tq, rc):
        scores_chunk(0, k0, sa_ref, r0)
    lax.fori_loop(0, nq, q_block, 0)


def _attn(proj, lq1, lk1, lq2, lk2, subln, batch, seq, lambda_init, cast_weights, *, tq=512, rc=256):
    n_tok = proj.shape[0]
    q_blk = 3 * (N_HEADS * DV) // LANES
    k_blk = q_blk + N_HEADS
    v_blk = k_blk + N_HEADS
    vec = lambda width: pl.BlockSpec((1, width), lambda b, h: (0, 0))
    head = lambda first_blk: pl.BlockSpec((seq, LANES), lambda b, h: (b, first_blk + h))
    n_grid = batch * N_HEADS
    for w in cast_weights:
        assert w.shape[0] % (n_grid * BF16_ROWS) == 0, w.shape
    slab = lambda w: pl.BlockSpec((w.shape[0] // n_grid, w.shape[1]), lambda b, h: (b * N_HEADS + h, 0))
    outs = pl.pallas_call(
        functools.partial(_attn_kernel, n_cast=len(cast_weights), tq=tq, rc=rc, lambda_init=lambda_init),
        grid=(batch, N_HEADS),
        in_specs=[vec(DK), vec(DK), vec(DK), vec(DK), vec(DV), head(q_blk), head(k_blk), head(v_blk)]
        + [slab(w) for w in cast_weights],
        out_specs=[head(0)] + [slab(w) for w in cast_weights],
        out_shape=[jax.ShapeDtypeStruct((n_tok, N_HEADS * DV), BF16)]
        + [jax.ShapeDtypeStruct(w.shape, BF16) for w in cast_weights],
        scratch_shapes=[
            pltpu.VMEM((2, 2 * tq, LANES), BF16),
            pltpu.VMEM((2 * tq, tq), F32),
            pltpu.VMEM((2 * tq, tq), F32),
            pltpu.VMEM((2 * tq, LANES), F32),
            pltpu.VMEM((2 * tq, LANES), F32),
            pltpu.VMEM((2 * tq, DV), F32),
        ],
        compiler_params=pltpu.CompilerParams(
            dimension_semantics=("arbitrary", "arbitrary"),
            vmem_limit_bytes=48 * MIB),
        name="diff_attn",
    )(lq1, lk1, lq2, lk2, subln, proj, proj, proj, *cast_weights)
    return outs[0], outs[1:]


def _conv_taps(cx_m2, cx_m1, cx, w):
    return w[0:1, :] * cx_m2 + w[1:2, :] * cx_m1 + w[2:3, :] * cx


def _out_proj_kernel(x_ref, b_ref, c_ref, xv_ref, ch_ref, xh_ref, cw_ref, ya_ref, wc_ref, wa_ref, o_ref,
                     yc_ref, *, tiles_per_seq):
    h = BF16_ROWS
    at_seq_start = pl.program_id(0) % tiles_per_seq == 0
    row = lax.broadcasted_iota(jnp.int32, (h, 1), 0)
    for c0 in range(0, cw_ref.shape[1], CONV_COLS):
        cols = slice(c0, c0 + CONV_COLS)
        w = cw_ref[:, cols]
        cx = c_ref[:, cols].astype(F32) * xv_ref[:, cols].astype(F32)
        y = _conv_taps(pltpu.roll(cx, 2, axis=0), pltpu.roll(cx, 1, axis=0), cx, w)
        yc_ref[:, cols] = (b_ref[:, cols].astype(F32) * y).astype(BF16)
        halo = ch_ref[:, cols].astype(F32) * xh_ref[:, cols].astype(F32)
        halo = jnp.where(at_seq_start, 0.0, halo)
        top = cx[:h]
        m1 = jnp.where(row < 1, pltpu.roll(halo, 1, axis=0), pltpu.roll(top, 1, axis=0))
        m2 = jnp.where(row < 2, pltpu.roll(halo, 2, axis=0), pltpu.roll(top, 2, axis=0))
        yc_ref[0:h, cols] = (b_ref[0:h, cols].astype(F32) * _conv_taps(m2, m1, top, w)).astype(BF16)
    y = jnp.dot(ya_ref[...], wa_ref[...], preferred_element_type=F32)
    y += jnp.dot(yc_ref[...], wc_ref[...], preferred_element_type=F32)
    o_ref[...] = x_ref[...] + y


def _out_proj(x, proj, y_attn, conv_w, w_out, seq, *, tm=512):
    n_tok, d = x.shape
    cw = conv_w.shape[1]
    aw = y_attn.shape[1]
    halo_blocks = tm // BF16_ROWS

    def halo_map(sec):
        return lambda i: (jnp.maximum(i * halo_blocks - 1, 0), sec)

    resident = dict(pipeline_mode=pl.Buffered(1))
    return pl.pallas_call(
        functools.partial(_out_proj_kernel, tiles_per_seq=seq // tm),
        grid=(n_tok // tm,),
        in_specs=[
            pl.BlockSpec((tm, d), lambda i: (i, 0)),
            pl.BlockSpec((tm, cw), lambda i: (i, 0)),
            pl.BlockSpec((tm, cw), lambda i: (i, 1)),
            pl.BlockSpec((tm, cw), lambda i: (i, 2)),
            pl.BlockSpec((BF16_ROWS, cw), halo_map(1)),
            pl.BlockSpec((BF16_ROWS, cw), halo_map(2)),
            pl.BlockSpec((CONV_K, cw), lambda i: (0, 0)),
            pl.BlockSpec((tm, aw), lambda i: (i, 0)),
            pl.BlockSpec((cw, d), lambda i: (0, 0), **resident),
            pl.BlockSpec((aw, d), lambda i: (1, 0), **resident),
        ],
        out_specs=pl.BlockSpec((tm, d), lambda i: (i, 0)),
        out_shape=jax.ShapeDtypeStruct((n_tok, d), F32),
        scratch_shapes=[pltpu.VMEM((tm, cw), BF16)],
        compiler_params=pltpu.CompilerParams(
            dimension_semantics=("arbitrary",),
            vmem_limit_bytes=48 * MIB),
        name="out_proj",
    )(x, proj, proj, proj, proj, proj, conv_w, y_attn, w_out, w_out)


def kernel(x, ffn1_norm, ffn1_w_gate, ffn1_w_up, ffn1_w_down, mix_norm, w_in, conv_w, lambda_q1, lambda_k1, lambda_q2, lambda_k2, subln_w, w_out, ffn2_norm, ffn2_w_gate, ffn2_w_up, ffn2_w_down, final_norm):
    batch, seq, d = x.shape
    depth = ffn1_norm.shape[0]
    assert w_in.shape[2] == 3 * conv_w.shape[2] + 3 * N_HEADS * DV
    assert conv_w.shape[2] == N_HEADS * DV
    h = x.reshape(batch * seq, d)
    row = lambda a: a.reshape(1, -1)
    for l in range(depth):
        last = l == depth - 1
        lambda_init = 0.8 - 0.6 * math.exp(-0.3 * l)
        h = _ffn(h, row(ffn1_norm[l]), ffn1_w_gate[l].astype(BF16), ffn1_w_up[l].astype(BF16),
                 ffn1_w_down[l].astype(BF16), row(final_norm), final_norm=False)
        proj = _in_proj(h, row(mix_norm[l]), w_in[l].astype(BF16))
        y_attn, (w_out_b, wg2, wu2, wd2) = _attn(
            proj, row(lambda_q1[l]), row(lambda_k1[l]), row(lambda_q2[l]), row(lambda_k2[l]), row(subln_w[l]),
            batch, seq, lambda_init, [w_out[l], ffn2_w_gate[l], ffn2_w_up[l], ffn2_w_down[l]])
        h = _out_proj(h, proj, y_attn, conv_w[l], w_out_b, seq)
        h = _ffn(h, row(ffn2_norm[l]), wg2, wu2, wd2, row(final_norm), final_norm=last)
    if depth == 0:
        raise ValueError("depth must be at least 1")
    return h.reshape(batch, seq, d)
```

```python
import functools
import math

import jax
import jax.numpy as jnp
from jax import lax
from jax.experimental import pallas as pl
from jax.experimental.pallas import tpu as pltpu

EPS = 1e-6
CONV_K = 3
N_HEADS = 8
DK = 64
DV = 128
LANES = 128
BF16_ROWS = 16
FFN_ROWS = 512
PROJ_ROWS = 256
CONV_COLS = 256
MIB = 1024 * 1024
VMEM_LIMIT = 48 * MIB
FFN_VMEM_LIMIT = 60 * MIB

F32 = jnp.float32
BF16 = jnp.bfloat16


def _rms(x, g):
    return x * lax.rsqrt(jnp.mean(x * x, axis=-1, keepdims=True) + EPS) * g


def _ffn_kernel(x_ref, g_ref, wg_ref, wu_ref, wd_ref, fg_ref, *rest, final_norm):
    rider_src, o_ref, rider_dst, n_ref = rest if len(rest) == 4 else (None, rest[0], None, rest[1])
    f = pl.program_id(1)
    last = pl.num_programs(1) - 1
    chunks = [slice(r0, r0 + FFN_ROWS) for r0 in range(0, x_ref.shape[0], FFN_ROWS)]

    def swiglu_down(n):
        gate = jnp.dot(n, wg_ref[...], preferred_element_type=F32)
        up = jnp.dot(n, wu_ref[...], preferred_element_type=F32)
        h = (gate * jax.nn.sigmoid(gate) * up).astype(BF16)
        return jnp.dot(h, wd_ref[...], preferred_element_type=F32)

    @pl.when(f == 0)
    def _():
        if rider_src is not None:
            rider_dst[...] = rider_src[...].astype(BF16)
        for rows in chunks:
            n = _rms(x_ref[rows, :], g_ref[...]).astype(BF16)
            n_ref[rows, :] = n
            o_ref[rows, :] = swiglu_down(n)

    @pl.when(jnp.logical_and(f > 0, f < last))
    def _():
        o_ref[...] += swiglu_down(n_ref[...])

    @pl.when(f == last)
    def _():
        for rows in chunks:
            y = x_ref[rows, :] + 0.5 * (o_ref[rows, :] + swiglu_down(n_ref[rows, :]))
            if final_norm:
                y = _rms(y, fg_ref[...])
            o_ref[rows, :] = y


def _ffn(x, gain, wg, wu, wd, final_gain, *, final_norm, rider=None, tm=1024, tf=512):
    n_tok, d = x.shape
    ffn = wg.shape[1]
    assert n_tok % tm == 0 and ffn % tf == 0
    assert ffn // tf >= 2
    in_specs = [
        pl.BlockSpec((tm, d), lambda i, f: (i, 0)),
        pl.BlockSpec((1, d), lambda i, f: (0, 0)),
        pl.BlockSpec((d, tf), lambda i, f: (0, f)),
        pl.BlockSpec((d, tf), lambda i, f: (0, f)),
        pl.BlockSpec((tf, d), lambda i, f: (f, 0)),
        pl.BlockSpec((1, d), lambda i, f: (0, 0)),
    ]
    out_specs = [pl.BlockSpec((tm, d), lambda i, f: (i, 0))]
    out_shape = [jax.ShapeDtypeStruct((n_tok, d), F32)]
    operands = [x, gain, wg, wu, wd, final_gain]
    if rider is not None:
        n_tiles = n_tok // tm
        assert rider.shape[0] % (n_tiles * BF16_ROWS) == 0, rider.shape
        slab = pl.BlockSpec((rider.shape[0] // n_tiles, rider.shape[1]), lambda i, f: (i, 0))
        in_specs.append(slab)
        out_specs.append(slab)
        out_shape.append(jax.ShapeDtypeStruct(rider.shape, BF16))
        operands.append(rider)
    outs = pl.pallas_call(
        functools.partial(_ffn_kernel, final_norm=final_norm),
        grid=(n_tok // tm, ffn // tf),
        in_specs=in_specs,
        out_specs=out_specs,
        out_shape=out_shape,
        scratch_shapes=[pltpu.VMEM((tm, d), BF16)],
        compiler_params=pltpu.CompilerParams(
            dimension_semantics=("arbitrary", "arbitrary"),
            vmem_limit_bytes=FFN_VMEM_LIMIT),
        name="ffn",
    )(*operands)
    return outs if rider is not None else outs[0]


def _in_proj_kernel(x_ref, g_ref, w_ref, o_ref, n_ref):
    j = pl.program_id(1)

    @pl.when(j == 0)
    def _():
        for r0 in range(0, x_ref.shape[0], PROJ_ROWS):
            rows = slice(r0, r0 + PROJ_ROWS)
            n = _rms(x_ref[rows, :], g_ref[...]).astype(BF16)
            n_ref[rows, :] = n
            o_ref[rows, :] = jnp.dot(n, w_ref[...], preferred_element_type=F32).astype(o_ref.dtype)

    @pl.when(j > 0)
    def _():
        o_ref[...] = jnp.dot(n_ref[...], w_ref[...], preferred_element_type=F32).astype(o_ref.dtype)


def _in_proj(x, gain, w, *, tm=1024, tn=2048):
    n_tok, d = x.shape
    cols = w.shape[1]
    assert n_tok % tm == 0 and cols % tn == 0
    return pl.pallas_call(
        _in_proj_kernel,
        grid=(n_tok // tm, cols // tn),
        in_specs=[
            pl.BlockSpec((tm, d), lambda i, j: (i, 0)),
            pl.BlockSpec((1, d), lambda i, j: (0, 0)),
            pl.BlockSpec((d, tn), lambda i, j: (0, j)),
        ],
        out_specs=pl.BlockSpec((tm, tn), lambda i, j: (i, j)),
        out_shape=jax.ShapeDtypeStruct((n_tok, cols), BF16),
        scratch_shapes=[pltpu.VMEM((tm, d), BF16)],
        compiler_params=pltpu.CompilerParams(
            dimension_semantics=("arbitrary", "arbitrary"),
            vmem_limit_bytes=VMEM_LIMIT),
        name="in_proj",
    )(x, gain, w)


def _attn_kernel(*refs, n_cast, tq, rc, lambda_init):
    lq1_ref, lk1_ref, lq2_ref, lk2_ref, sw_ref, q_ref, k_ref, v_ref = refs[:8]
    o_ref = refs[8 + n_cast]
    q2_ref, sa_ref, sb_ref, m_ref, l_ref, acc_ref = refs[9 + 2 * n_cast:]
    for src_ref, dst_ref in zip(refs[8:8 + n_cast], refs[9 + n_cast:9 + 2 * n_cast]):
        dst_ref[...] = src_ref[...].astype(BF16)

    seq = q_ref.shape[0]
    nq = seq // tq
    scale = 1.0 / math.sqrt(DK)
    contract_last = (((1,), (1,)), ((), ()))

    def block(idx):
        return pl.ds(pl.multiple_of(idx * tq, tq), tq)

    def prep_q(i, slot):
        q = q_ref[block(i), :]
        lane = lax.broadcasted_iota(jnp.int32, q.shape, 1)
        qs = (q.astype(F32) * scale).astype(BF16)
        zero = jnp.zeros_like(qs)
        q2_ref[slot, 0:tq, :] = jnp.where(lane < DK, qs, zero)
        q2_ref[slot, tq:2 * tq, :] = jnp.where(lane >= DK, qs, zero)

    def reset_stats():
        m_ref[...] = jnp.full_like(m_ref, -jnp.inf)
        l_ref[...] = jnp.zeros_like(l_ref)
        acc_ref[...] = jnp.zeros_like(acc_ref)

    def scores_chunk(q_slot, k, s_ref, r0):
        s_ref[r0:r0 + rc, :] = lax.dot_general(
            q2_ref[q_slot, r0:r0 + rc, :], k, contract_last, preferred_element_type=F32)

    def update_chunk(s_ref, v_ones, r0, masked):
        s = s_ref[r0:r0 + rc, :]
        if masked:
            q_pos = lax.broadcasted_iota(jnp.int32, (rc, tq), 0) + (r0 % tq)
            k_pos = lax.broadcasted_iota(jnp.int32, (rc, tq), 1)
            s = jnp.where(k_pos <= q_pos, s, -jnp.inf)
        cols = [s[:, c0:c0 + LANES] for c0 in range(0, tq, LANES)]
        lane_max = functools.reduce(jnp.maximum, cols)
        row_max = jnp.broadcast_to(jnp.max(lane_max, axis=-1, keepdims=True), (rc, LANES))
        rows = pl.ds(r0, rc)
        m_prev = m_ref[rows, :]
        m_new = jnp.maximum(m_prev, row_max)
        alpha = jnp.exp(m_prev - m_new)
        p = jnp.concatenate([jnp.exp(col - m_new) for col in cols], axis=1).astype(BF16)
        pv = jnp.dot(p, v_ones, preferred_element_type=F32)
        acc_ref[rows, :] = alpha * acc_ref[rows, :] + pv[:, :DV]
        l_ref[rows, :] = alpha * l_ref[rows, :] + pv[:, DV:]
        m_ref[rows, :] = m_new

    def step(j, q_slot_next, j_next, s_cur, s_next, masked):
        k_next = k_ref[block(j_next), :]
        v = v_ref[block(j), :]
        v_ones = jnp.concatenate([v, jnp.ones_like(v)], axis=1)
        for r0 in range(0, 2 * tq, rc):
            if s_next is s_cur:
                update_chunk(s_cur, v_ones, r0, masked)
                scores_chunk(q_slot_next, k_next, s_next, r0)
            else:
                scores_chunk(q_slot_next, k_next, s_next, r0)
                update_chunk(s_cur, v_ones, r0, masked)

    def finish(i):
        lam = (jnp.exp(jnp.sum(lq1_ref[...] * lk1_ref[...], axis=-1, keepdims=True))
               - jnp.exp(jnp.sum(lq2_ref[...] * lk2_ref[...], axis=-1, keepdims=True))
               + lambda_init)
        o_all = acc_ref[...] / l_ref[...]
        o = o_all[0:tq] - lam * o_all[tq:2 * tq]
        o_ref[block(i), :] = (_rms(o, sw_ref[...]) * (1.0 - lambda_init)).astype(o_ref.dtype)

    def straight_steps(i, j, count):
        for u in range(0, count, 2):
            step(j + u, i % 2, j + u + 1, sa_ref, sb_ref, masked=False)
            step(j + u + 1, i % 2, j + u + 2, sb_ref, sa_ref, masked=False)

    def q_block(i, carry):
        def octet(g, c):
            straight_steps(i, 8 * g, 8)
            return c

        n_octets = i // 8
        lax.fori_loop(0, n_octets, octet, 0)
        n_quads = i // 4

        @pl.when(n_quads % 2 == 1)
        def _():
            straight_steps(i, 8 * n_octets, 4)

        rest = i - 4 * n_quads
        q_slot_after = jnp.minimum(i + 1, nq - 1) % 2

        for tail in range(4):
            @pl.when(rest == tail)
            def _(tail=tail):
                straight_steps(i, 4 * n_quads, tail - tail % 2)
                prep_q(jnp.minimum(i + 1, nq - 1), (i + 1) % 2)
                if tail % 2 == 1:
                    step(i - 1, i % 2, i, sa_ref, sb_ref, masked=False)
                    step(i, q_slot_after, 0, sb_ref, sa_ref, masked=True)
                else:
                    step(i, q_slot_after, 0, sa_ref, sa_ref, masked=True)
                finish(i)
                reset_stats()

        return carry

    prep_q(0, 0)
    reset_stats()
    k0 = k_ref[block(0), :]
    for r0 in range(0, 2 * tq, rc):
        scores_chunk(0, k0, sa_ref, r0)
    lax.fori_loop(0, nq, q_block, 0)


def _attn(proj, lq1, lk1, lq2, lk2, subln, batch, seq, lambda_init, cast_weights, *, tq=512, rc=256):
    n_tok = proj.shape[0]
    assert seq % tq == 0 and (2 * tq) % rc == 0
    q_blk = 3 * (N_HEADS * DV) // LANES
    k_blk = q_blk + N_HEADS
    v_blk = k_blk + N_HEADS
    vec = lambda width: pl.BlockSpec((1, width), lambda b, h: (0, 0))
    head = lambda first_blk: pl.BlockSpec((seq, LANES), lambda b, h: (b, first_blk + h))
    n_grid = batch * N_HEADS
    for w in cast_weights:
        assert w.shape[0] % (n_grid * BF16_ROWS) == 0, w.shape
    slab = lambda w: pl.BlockSpec((w.shape[0] // n_grid, w.shape[1]), lambda b, h: (b * N_HEADS + h, 0))
    outs = pl.pallas_call(
        functools.partial(_attn_kernel, n_cast=len(cast_weights), tq=tq, rc=rc, lambda_init=lambda_init),
        grid=(batch, N_HEADS),
        in_specs=[vec(DK), vec(DK), vec(DK), vec(DK), vec(DV), head(q_blk), head(k_blk), head(v_blk)]
        + [slab(w) for w in cast_weights],
        out_specs=[head(0)] + [slab(w) for w in cast_weights],
        out_shape=[jax.ShapeDtypeStruct((n_tok, N_HEADS * DV), BF16)]
        + [jax.ShapeDtypeStruct(w.shape, BF16) for w in cast_weights],
        scratch_shapes=[
            pltpu.VMEM((2, 2 * tq, LANES), BF16),
            pltpu.VMEM((2 * tq, tq), F32),
            pltpu.VMEM((2 * tq, tq), F32),
            pltpu.VMEM((2 * tq, LANES), F32),
            pltpu.VMEM((2 * tq, LANES), F32),
            pltpu.VMEM((2 * tq, DV), F32),
        ],
        compiler_params=pltpu.CompilerParams(
            dimension_semantics=("arbitrary", "arbitrary"),
            vmem_limit_bytes=VMEM_LIMIT),
        name="diff_attn",
    )(lq1, lk1, lq2, lk2, subln, proj, proj, proj, *cast_weights)
    return outs[0], outs[1:]


def _conv_taps(cx_m2, cx_m1, cx, w):
    return w[0:1, :] * cx_m2 + w[1:2, :] * cx_m1 + w[2:3, :] * cx


def _out_proj_kernel(x_ref, b_ref, c_ref, xv_ref, ch_ref, xh_ref, cw_ref, ya_ref, wc_ref, wa_ref, o_ref,
                     yc_ref, *, tiles_per_seq):
    h = BF16_ROWS
    at_seq_start = pl.program_id(0) % tiles_per_seq == 0
    row = lax.broadcasted_iota(jnp.int32, (h, 1), 0)
    for c0 in range(0, cw_ref.shape[1], CONV_COLS):
        cols = slice(c0, c0 + CONV_COLS)
        w = cw_ref[:, cols]
        cx = c_ref[:, cols].astype(F32) * xv_ref[:, cols].astype(F32)
        y = _conv_taps(pltpu.roll(cx, 2, axis=0), pltpu.roll(cx, 1, axis=0), cx, w)
        yc_ref[:, cols] = (b_ref[:, cols].astype(F32) * y).astype(BF16)
        halo = ch_ref[:, cols].astype(F32) * xh_ref[:, cols].astype(F32)
        halo = jnp.where(at_seq_start, 0.0, halo)
        top = cx[:h]
        m1 = jnp.where(row < 1, pltpu.roll(halo, 1, axis=0), pltpu.roll(top, 1, axis=0))
        m2 = jnp.where(row < 2, pltpu.roll(halo, 2, axis=0), pltpu.roll(top, 2, axis=0))
        yc_ref[0:h, cols] = (b_ref[0:h, cols].astype(F32) * _conv_taps(m2, m1, top, w)).astype(BF16)
    y = jnp.dot(ya_ref[...], wa_ref[...], preferred_element_type=F32)
    y += jnp.dot(yc_ref[...], wc_ref[...], preferred_element_type=F32)
    o_ref[...] = x_ref[...] + y


def _out_proj(x, proj, y_attn, conv_w, w_out, seq, *, tm=512):
    n_tok, d = x.shape
    cw = conv_w.shape[1]
    aw = y_attn.shape[1]
    assert seq % tm == 0 and cw % CONV_COLS == 0
    halo_blocks = tm // BF16_ROWS

    def halo_map(sec):
        return lambda i: (jnp.maximum(i * halo_blocks - 1, 0), sec)

    resident = dict(pipeline_mode=pl.Buffered(1))
    return pl.pallas_call(
        functools.partial(_out_proj_kernel, tiles_per_seq=seq // tm),
        grid=(n_tok // tm,),
        in_specs=[
            pl.BlockSpec((tm, d), lambda i: (i, 0)),
            pl.BlockSpec((tm, cw), lambda i: (i, 0)),
            pl.BlockSpec((tm, cw), lambda i: (i, 1)),
            pl.BlockSpec((tm, cw), lambda i: (i, 2)),
            pl.BlockSpec((BF16_ROWS, cw), halo_map(1)),
            pl.BlockSpec((BF16_ROWS, cw), halo_map(2)),
            pl.BlockSpec((CONV_K, cw), lambda i: (0, 0)),
            pl.BlockSpec((tm, aw), lambda i: (i, 0)),
            pl.BlockSpec((cw, d), lambda i: (0, 0), **resident),
            pl.BlockSpec((aw, d), lambda i: (1, 0), **resident),
        ],
        out_specs=pl.BlockSpec((tm, d), lambda i: (i, 0)),
        out_shape=jax.ShapeDtypeStruct((n_tok, d), F32),
        scratch_shapes=[pltpu.VMEM((tm, cw), BF16)],
        compiler_params=pltpu.CompilerParams(
            dimension_semantics=("arbitrary",),
            vmem_limit_bytes=VMEM_LIMIT),
        name="out_proj",
    )(x, proj, proj, proj, proj, proj, conv_w, y_attn, w_out, w_out)


def kernel(x, ffn1_norm, ffn1_w_gate, ffn1_w_up, ffn1_w_down, mix_norm, w_in, conv_w, lambda_q1, lambda_k1, lambda_q2, lambda_k2, subln_w, w_out, ffn2_norm, ffn2_w_gate, ffn2_w_up, ffn2_w_down, final_norm):
    batch, seq, d = x.shape
    depth = ffn1_norm.shape[0]
    assert w_in.shape[2] == 3 * conv_w.shape[2] + 3 * N_HEADS * DV
    assert conv_w.shape[2] == N_HEADS * DV
    h = x.reshape(batch * seq, d)
    row = lambda a: a.reshape(1, -1)
    for l in range(depth):
        last = l == depth - 1
        lambda_init = 0.8 - 0.6 * math.exp(-0.3 * l)
        h, w_in_b = _ffn(h, row(ffn1_norm[l]), ffn1_w_gate[l].astype(BF16), ffn1_w_up[l].astype(BF16),
                         ffn1_w_down[l].astype(BF16), row(final_norm), final_norm=False, rider=w_in[l])
        proj = _in_proj(h, row(mix_norm[l]), w_in_b)
        y_attn, (w_out_b, wg2, wu2, wd2) = _attn(
            proj, row(lambda_q1[l]), row(lambda_k1[l]), row(lambda_q2[l]), row(lambda_k2[l]), row(subln_w[l]),
            batch, seq, lambda_init, [w_out[l], ffn2_w_gate[l], ffn2_w_up[l], ffn2_w_down[l]])
        h = _out_proj(h, proj, y_attn, conv_w[l], w_out_b, seq)
        h = _ffn(h, row(ffn2_norm[l]), wg2, wu2, wd2, row(final_norm), final_norm=last)
    if depth == 0:
        raise ValueError("depth must be at least 1")
    return h.reshape(batch, seq, d)
```

```python
import functools
import math

import jax
import jax.numpy as jnp
from jax import lax
from jax.experimental import pallas as pl
from jax.experimental.pallas import tpu as pltpu

EPS = 1e-6
CONV_K = 3
N_HEADS = 8
DK = 64
DV = 128
LANES = 128
BF16_ROWS = 16
FFN_ROWS = 512
PROJ_ROWS = 256
CONV_COLS = 256
MIB = 1024 * 1024
VMEM_LIMIT = 48 * MIB
FFN_VMEM_LIMIT = 60 * MIB

F32 = jnp.float32
BF16 = jnp.bfloat16


def _rms(x, g):
    return x * lax.rsqrt(jnp.mean(x * x, axis=-1, keepdims=True) + EPS) * g


def _ffn_kernel(x_ref, g_ref, wg_ref, wu_ref, wd_ref, fg_ref, *rest, final_norm):
    rider_src, o_ref, rider_dst, n_ref = rest if len(rest) == 4 else (None, rest[0], None, rest[1])
    f = pl.program_id(1)
    last = pl.num_programs(1) - 1
    chunks = [slice(r0, r0 + FFN_ROWS) for r0 in range(0, x_ref.shape[0], FFN_ROWS)]

    def swiglu_down(n):
        gate = jnp.dot(n, wg_ref[...], preferred_element_type=F32)
        up = jnp.dot(n, wu_ref[...], preferred_element_type=F32)
        h = (gate * jax.nn.sigmoid(gate) * up).astype(BF16)
        return jnp.dot(h, wd_ref[...], preferred_element_type=F32)

    @pl.when(f == 0)
    def _():
        if rider_src is not None:
            rider_dst[...] = rider_src[...].astype(BF16)
        for rows in chunks:
            n = _rms(x_ref[rows, :], g_ref[...]).astype(BF16)
            n_ref[rows, :] = n
            o_ref[rows, :] = swiglu_down(n)

    @pl.when(jnp.logical_and(f > 0, f < last))
    def _():
        o_ref[...] += swiglu_down(n_ref[...])

    @pl.when(f == last)
    def _():
        for rows in chunks:
            y = x_ref[rows, :] + 0.5 * (o_ref[rows, :] + swiglu_down(n_ref[rows, :]))
            if final_norm:
                y = _rms(y, fg_ref[...])
            o_ref[rows, :] = y


def _ffn(x, gain, wg, wu, wd, final_gain, *, final_norm, rider=None, tm=1024, tf=512):
    n_tok, d = x.shape
    ffn = wg.shape[1]
    assert n_tok % tm == 0 and ffn % tf == 0
    assert ffn // tf >= 2
    in_specs = [
        pl.BlockSpec((tm, d), lambda i, f: (i, 0)),
        pl.BlockSpec((1, d), lambda i, f: (0, 0)),
        pl.BlockSpec((d, tf), lambda i, f: (0, f)),
        pl.BlockSpec((d, tf), lambda i, f: (0, f)),
        pl.BlockSpec((tf, d), lambda i, f: (f, 0)),
        pl.BlockSpec((1, d), lambda i, f: (0, 0)),
    ]
    out_specs = [pl.BlockSpec((tm, d), lambda i, f: (i, 0))]
    out_shape = [jax.ShapeDtypeStruct((n_tok, d), F32)]
    operands = [x, gain, wg, wu, wd, final_gain]
    if rider is not None:
        n_tiles = n_tok // tm
        assert rider.shape[0] % (n_tiles * BF16_ROWS) == 0, rider.shape
        slab = pl.BlockSpec((rider.shape[0] // n_tiles, rider.shape[1]), lambda i, f: (i, 0))
        in_specs.append(slab)
        out_specs.append(slab)
        out_shape.append(jax.ShapeDtypeStruct(rider.shape, BF16))
        operands.append(rider)
    outs = pl.pallas_call(
        functools.partial(_ffn_kernel, final_norm=final_norm),
        grid=(n_tok // tm, ffn // tf),
        in_specs=in_specs,
        out_specs=out_specs,
        out_shape=out_shape,
        scratch_shapes=[pltpu.VMEM((tm, d), BF16)],
        compiler_params=pltpu.CompilerParams(
            dimension_semantics=("arbitrary", "arbitrary"),
            vmem_limit_bytes=FFN_VMEM_LIMIT),
        name="ffn",
    )(*operands)
    return outs if rider is not None else outs[0]


def _in_proj_kernel(x_ref, g_ref, w_ref, o_ref, n_ref):
    j = pl.program_id(1)

    @pl.when(j == 0)
    def _():
        for r0 in range(0, x_ref.shape[0], PROJ_ROWS):
            rows = slice(r0, r0 + PROJ_ROWS)
            n = _rms(x_ref[rows, :], g_ref[...]).astype(BF16)
            n_ref[rows, :] = n
            o_ref[rows, :] = jnp.dot(n, w_ref[...], preferred_element_type=F32).astype(o_ref.dtype)

    @pl.when(j > 0)
    def _():
        o_ref[...] = jnp.dot(n_ref[...], w_ref[...], preferred_element_type=F32).astype(o_ref.dtype)


def _in_proj(x, gain, w, *, tm=1024, tn=2048):
    n_tok, d = x.shape
    cols = w.shape[1]
    assert n_tok % tm == 0 and cols % tn == 0
    return pl.pallas_call(
        _in_proj_kernel,
        grid=(n_tok // tm, cols // tn),
        in_specs=[
            pl.BlockSpec((tm, d), lambda i, j: (i, 0)),
            pl.BlockSpec((1, d), lambda i, j: (0, 0)),
            pl.BlockSpec((d, tn), lambda i, j: (0, j)),
        ],
        out_specs=pl.BlockSpec((tm, tn), lambda i, j: (i, j)),
        out_shape=jax.ShapeDtypeStruct((n_tok, cols), BF16),
        scratch_shapes=[pltpu.VMEM((tm, d), BF16)],
        compiler_params=pltpu.CompilerParams(
            dimension_semantics=("arbitrary", "arbitrary"),
            vmem_limit_bytes=VMEM_LIMIT),
        name="in_proj",
    )(x, gain, w)


def _attn_kernel(*refs, n_cast, tq, rc, lambda_init):
    lq1_ref, lk1_ref, lq2_ref, lk2_ref, sw_ref, q_ref, k_ref, v_ref = refs[:8]
    o_ref = refs[8 + n_cast]
    q2_ref, sa_ref, sb_ref, m_ref, l_ref, acc_ref = refs[9 + 2 * n_cast:]
    for src_ref, dst_ref in zip(refs[8:8 + n_cast], refs[9 + n_cast:9 + 2 * n_cast]):
        dst_ref[...] = src_ref[...].astype(BF16)

    seq = q_ref.shape[0]
    nq = seq // tq
    scale = 1.0 / math.sqrt(DK)
    contract_last = (((1,), (1,)), ((), ()))

    def block(idx):
        return pl.ds(pl.multiple_of(idx * tq, tq), tq)

    def prep_q(i, slot):
        q = q_ref[block(i), :]
        lane = lax.broadcasted_iota(jnp.int32, q.shape, 1)
        qs = (q.astype(F32) * scale).astype(BF16)
        zero = jnp.zeros_like(qs)
        q2_ref[slot, 0:tq, :] = jnp.where(lane < DK, qs, zero)
        q2_ref[slot, tq:2 * tq, :] = jnp.where(lane >= DK, qs, zero)

    def reset_stats():
        m_ref[...] = jnp.full_like(m_ref, -jnp.inf)
        l_ref[...] = jnp.zeros_like(l_ref)
        acc_ref[...] = jnp.zeros_like(acc_ref)

    def scores_chunk(q_slot, k, s_ref, r0):
        s_ref[r0:r0 + rc, :] = lax.dot_general(
            q2_ref[q_slot, r0:r0 + rc, :], k, contract_last, preferred_element_type=F32)

    def update_chunk(s_ref, v_ones, r0, masked):
        s = s_ref[r0:r0 + rc, :]
        if masked:
            q_pos = lax.broadcasted_iota(jnp.int32, (rc, tq), 0) + (r0 % tq)
            k_pos = lax.broadcasted_iota(jnp.int32, (rc, tq), 1)
            s = jnp.where(k_pos <= q_pos, s, -jnp.inf)
        cols = [s[:, c0:c0 + LANES] for c0 in range(0, tq, LANES)]
        lane_max = functools.reduce(jnp.maximum, cols)
        row_max = jnp.broadcast_to(jnp.max(lane_max, axis=-1, keepdims=True), (rc, LANES))
        rows = pl.ds(r0, rc)
        m_prev = m_ref[rows, :]
        m_new = jnp.maximum(m_prev, row_max)
        alpha = jnp.exp(m_prev - m_new)
        p = jnp.concatenate([jnp.exp(col - m_new) for col in cols], axis=1).astype(BF16)
        pv = jnp.dot(p, v_ones, preferred_element_type=F32)
        acc_ref[rows, :] = alpha * acc_ref[rows, :] + pv[:, :DV]
        l_ref[rows, :] = alpha * l_ref[rows, :] + pv[:, DV:]
        m_ref[rows, :] = m_new

    def step(j, q_slot_next, j_next, s_cur, s_next, masked):
        k_next = k_ref[block(j_next), :]
        v = v_ref[block(j), :]
        v_ones = jnp.concatenate([v, jnp.ones_like(v)], axis=1)
        for r0 in range(0, 2 * tq, rc):
            if s_next is s_cur:
                update_chunk(s_cur, v_ones, r0, masked)
                scores_chunk(q_slot_next, k_next, s_next, r0)
            else:
                scores_chunk(q_slot_next, k_next, s_next, r0)
                update_chunk(s_cur, v_ones, r0, masked)

    def finish(i):
        lam = (jnp.exp(jnp.sum(lq1_ref[...] * lk1_ref[...], axis=-1, keepdims=True))
               - jnp.exp(jnp.sum(lq2_ref[...] * lk2_ref[...], axis=-1, keepdims=True))
               + lambda_init)
        o_all = acc_ref[...] / l_ref[...]
        o = o_all[0:tq] - lam * o_all[tq:2 * tq]
        o_ref[block(i), :] = (_rms(o, sw_ref[...]) * (1.0 - lambda_init)).astype(o_ref.dtype)

    def straight_steps(i, j, count):
        for u in range(0, count, 2):
            step(j + u, i % 2, j + u + 1, sa_ref, sb_ref, masked=False)
            step(j + u + 1, i % 2, j + u + 2, sb_ref, sa_ref, masked=False)

    def q_block(i, carry):
        def octet(g, c):
            straight_steps(i, 8 * g, 8)
            return c

        n_octets = i // 8
        lax.fori_loop(0, n_octets, octet, 0)
        rest = i - 8 * n_octets
        q_slot_after = jnp.minimum(i + 1, nq - 1) % 2

        for tail in range(8):
            @pl.when(rest == tail)
            def _(tail=tail):
                straight_steps(i, 8 * n_octets, tail - tail % 2)
                prep_q(jnp.minimum(i + 1, nq - 1), (i + 1) % 2)
                if tail % 2 == 1:
                    step(i - 1, i % 2, i, sa_ref, sb_ref, masked=False)
                    step(i, q_slot_after, 0, sb_ref, sa_ref, masked=True)
                else:
                    step(i, q_slot_after, 0, sa_ref, sa_ref, masked=True)
                finish(i)
                reset_stats()

        return carry

    prep_q(0, 0)
    reset_stats()
    k0 = k_ref[block(0), :]
    for r0 in range(0, 2 * tq, rc):
        scores_chunk(0, k0, sa_ref, r0)
    lax.fori_loop(0, nq, q_block, 0)


def _attn(proj, lq1, lk1, lq2, lk2, subln, batch, seq, lambda_init, cast_weights, *, tq=512, rc=256):
    n_tok = proj.shape[0]
    assert seq % tq == 0 and (2 * tq) % rc == 0
    q_blk = 3 * (N_HEADS * DV) // LANES
    k_blk = q_blk + N_HEADS
    v_blk = k_blk + N_HEADS
    vec = lambda width: pl.BlockSpec((1, width), lambda b, h: (0, 0))
    head = lambda first_blk: pl.BlockSpec((seq, LANES), lambda b, h: (b, first_blk + h))
    n_grid = batch * N_HEADS
    for w in cast_weights:
        assert w.shape[0] % (n_grid * BF16_ROWS) == 0, w.shape
    slab = lambda w: pl.BlockSpec((w.shape[0] // n_grid, w.shape[1]), lambda b, h: (b * N_HEADS + h, 0))
    outs = pl.pallas_call(
        functools.partial(_attn_kernel, n_cast=len(cast_weights), tq=tq, rc=rc, lambda_init=lambda_init),
        grid=(batch, N_HEADS),
        in_specs=[vec(DK), vec(DK), vec(DK), vec(DK), vec(DV), head(q_blk), head(k_blk), head(v_blk)]
        + [slab(w) for w in cast_weights],
        out_specs=[head(0)] + [slab(w) for w in cast_weights],
        out_shape=[jax.ShapeDtypeStruct((n_tok, N_HEADS * DV), BF16)]
        + [jax.ShapeDtypeStruct(w.shape, BF16) for w in cast_weights],
        scratch_shapes=[
            pltpu.VMEM((2, 2 * tq, LANES), BF16),
            pltpu.VMEM((2 * tq, tq), F32),
            pltpu.VMEM((2 * tq, tq), F32),
            pltpu.VMEM((2 * tq, LANES), F32),
            pltpu.VMEM((2 * tq, LANES), F32),
            pltpu.VMEM((2 * tq, DV), F32),
        ],
        compiler_params=pltpu.CompilerParams(
            dimension_semantics=("arbitrary", "arbitrary"),
            vmem_limit_bytes=VMEM_LIMIT),
        name="diff_attn",
    )(lq1, lk1, lq2, lk2, subln, proj, proj, proj, *cast_weights)
    return outs[0], outs[1:]


def _conv_taps(cx_m2, cx_m1, cx, w):
    return w[0:1, :] * cx_m2 + w[1:2, :] * cx_m1 + w[2:3, :] * cx


def _out_proj_kernel(x_ref, b_ref, c_ref, xv_ref, ch_ref, xh_ref, cw_ref, ya_ref, wc_ref, wa_ref, o_ref,
                     yc_ref, *, tiles_per_seq):
    h = BF16_ROWS
    at_seq_start = pl.program_id(0) % tiles_per_seq == 0
    row = lax.broadcasted_iota(jnp.int32, (h, 1), 0)
    for c0 in range(0, cw_ref.shape[1], CONV_COLS):
        cols = slice(c0, c0 + CONV_COLS)
        w = cw_ref[:, cols]
        cx = c_ref[:, cols].astype(F32) * xv_ref[:, cols].astype(F32)
        y = _conv_taps(pltpu.roll(cx, 2, axis=0), pltpu.roll(cx, 1, axis=0), cx, w)
        yc_ref[:, cols] = (b_ref[:, cols].astype(F32) * y).astype(BF16)
        halo = ch_ref[:, cols].astype(F32) * xh_ref[:, cols].astype(F32)
        halo = jnp.where(at_seq_start, 0.0, halo)
        top = cx[:h]
        m1 = jnp.where(row < 1, pltpu.roll(halo, 1, axis=0), pltpu.roll(top, 1, axis=0))
        m2 = jnp.where(row < 2, pltpu.roll(halo, 2, axis=0), pltpu.roll(top, 2, axis=0))
        yc_ref[0:h, cols] = (b_ref[0:h, cols].astype(F32) * _conv_taps(m2, m1, top, w)).astype(BF16)
    y = jnp.dot(ya_ref[...], wa_ref[...], preferred_element_type=F32)
    y += jnp.dot(yc_ref[...], wc_ref[...], preferred_element_type=F32)
    o_ref[...] = x_ref[...] + y


def _out_proj(x, proj, y_attn, conv_w, w_out, seq, *, tm=512):
    n_tok, d = x.shape
    cw = conv_w.shape[1]
    aw = y_attn.shape[1]
    assert seq % tm == 0 and cw % CONV_COLS == 0
    halo_blocks = tm // BF16_ROWS

    def halo_map(sec):
        return lambda i: (jnp.maximum(i * halo_blocks - 1, 0), sec)

    resident = dict(pipeline_mode=pl.Buffered(1))
    return pl.pallas_call(
        functools.partial(_out_proj_kernel, tiles_per_seq=seq // tm),
        grid=(n_tok // tm,),
        in_specs=[
            pl.BlockSpec((tm, d), lambda i: (i, 0)),
            pl.BlockSpec((tm, cw), lambda i: (i, 0)),
            pl.BlockSpec((tm, cw), lambda i: (i, 1)),
            pl.BlockSpec((tm, cw), lambda i: (i, 2)),
            pl.BlockSpec((BF16_ROWS, cw), halo_map(1)),
            pl.BlockSpec((BF16_ROWS, cw), halo_map(2)),
            pl.BlockSpec((CONV_K, cw), lambda i: (0, 0)),
            pl.BlockSpec((tm, aw), lambda i: (i, 0)),
            pl.BlockSpec((cw, d), lambda i: (0, 0), **resident),
            pl.BlockSpec((aw, d), lambda i: (1, 0), **resident),
        ],
        out_specs=pl.BlockSpec((tm, d), lambda i: (i, 0)),
        out_shape=jax.ShapeDtypeStruct((n_tok, d), F32),
        scratch_shapes=[pltpu.VMEM((tm, cw), BF16)],
        compiler_params=pltpu.CompilerParams(
            dimension_semantics=("arbitrary",),
            vmem_limit_bytes=VMEM_LIMIT),
        name="out_proj",
    )(x, proj, proj, proj, proj, proj, conv_w, y_attn, w_out, w_out)


def kernel(x, ffn1_norm, ffn1_w_gate, ffn1_w_up, ffn1_w_down, mix_norm, w_in, conv_w, lambda_q1, lambda_k1, lambda_q2, lambda_k2, subln_w, w_out, ffn2_norm, ffn2_w_gate, ffn2_w_up, ffn2_w_down, final_norm):
    batch, seq, d = x.shape
    depth = ffn1_norm.shape[0]
    assert w_in.shape[2] == 3 * conv_w.shape[2] + 3 * N_HEADS * DV
    assert conv_w.shape[2] == N_HEADS * DV
    h = x.reshape(batch * seq, d)
    row = lambda a: a.reshape(1, -1)
    for l in range(depth):
        last = l == depth - 1
        lambda_init = 0.8 - 0.6 * math.exp(-0.3 * l)
        h, w_in_b = _ffn(h, row(ffn1_norm[l]), ffn1_w_gate[l].astype(BF16), ffn1_w_up[l].astype(BF16),
                         ffn1_w_down[l].astype(BF16), row(final_norm), final_norm=False, rider=w_in[l])
        proj = _in_proj(h, row(mix_norm[l]), w_in_b)
        y_attn, (w_out_b, wg2, wu2, wd2) = _attn(
            proj, row(lambda_q1[l]), row(lambda_k1[l]), row(lambda_q2[l]), row(lambda_k2[l]), row(subln_w[l]),
            batch, seq, lambda_init, [w_out[l], ffn2_w_gate[l], ffn2_w_up[l], ffn2_w_down[l]])
        h = _out_proj(h, proj, y_attn, conv_w[l], w_out_b, seq)
        h = _ffn(h, row(ffn2_norm[l]), wg2, wu2, wd2, row(final_norm), final_norm=last)
    if depth == 0:
        raise ValueError("depth must be at least 1")
    return h.reshape(batch, seq, d)
```
